```python
import jax, jax.numpy as jnp
from jax import lax
import numpy as np

D_MODEL = 2048
BATCH = 8
SEQ = 2048
DEPTH = 1

CTX_LEN = 256
GRID_W = 64
D_MIX = D_MODEL
NA_HEADS = 8
HEAD_DIM = 128
D_ATTN = NA_HEADS * HEAD_DIM
D_CONV = D_MIX - D_ATTN
CONV_GROUPS = 8
CONV_W = 3
WIN_H = 8
WIN_W = 16
PEER_HEADS = 8
PEER_TOPK = 16
N_KEYS = 128
N_EXPERTS = N_KEYS * N_KEYS
PEER_QDIM = 256
PEER_BLOCK = 128
N_MOD = 6
EPS = 1e-6
NEG_INF = -1e30

kernel_name = "hymba_natten_shortconv_peer_dit"


def rms_norm(x, g):
    xf = x.astype(jnp.float32)
    y = xf * lax.rsqrt(jnp.mean(xf * xf, axis=-1, keepdims=True) + EPS)
    return (y * g.astype(jnp.float32)).astype(x.dtype)


def modulate(h, shift, scale):
    return h * (1 + scale) + shift


def split_proj(p):
    return jnp.split(p, [D_ATTN, 2 * D_ATTN, 3 * D_ATTN, 3 * D_ATTN + D_CONV, 3 * D_ATTN + 2 * D_CONV], axis=-1)


def to_heads(t):
    b, n, _ = t.shape
    return t.reshape(b, n, NA_HEADS, HEAD_DIM).transpose(0, 2, 1, 3)


def from_heads(t):
    b, h, n, d = t.shape
    return t.transpose(0, 2, 1, 3).reshape(b, n, h * d)


def group_rms(y, g):
    gs, dg = g.shape
    return rms_norm(y.reshape(*y.shape[:-1], gs, dg), g).reshape(y.shape)


def short_conv(u, b_gate, c_gate, w_conv):
    z = c_gate * u
    y = lax.conv_general_dilated(z, w_conv[:, None, :], window_strides=(1,),
                                 padding=((CONV_W // 2, CONV_W // 2),),
                                 dimension_numbers=('NWC', 'WIO', 'NWC'),
                                 feature_group_count=D_CONV)
    return b_gate * y


def neighbourhood_attention(q, k, v, k_ctx, v_ctx, rpb):
    b, h, n, dh = q.shape
    rows = n // GRID_W
    kh = min(WIN_H, rows)
    scale = dh ** -0.5
    qg = q.reshape(b, h, rows, GRID_W, dh)
    kg = k.reshape(b, h, rows, GRID_W, dh)
    vg = v.reshape(b, h, rows, GRID_W, dh)
    col = jnp.arange(GRID_W)
    col_start = jnp.clip(col - WIN_W // 2, 0, GRID_W - WIN_W)
    col_in = (col[None, :] >= col_start[:, None]) & (col[None, :] < col_start[:, None] + WIN_W)
    dc = jnp.clip(col[None, :] - col[:, None], -(WIN_W - 1), WIN_W - 1) + (WIN_W - 1)
    rpb32 = rpb.astype(jnp.float32)

    def row_block(r):
        rs = jnp.clip(r - kh // 2, 0, rows - kh)
        q_blk = lax.dynamic_index_in_dim(qg, r, axis=2, keepdims=False)
        k_blk = lax.dynamic_slice_in_dim(kg, rs, kh, axis=2)
        v_blk = lax.dynamic_slice_in_dim(vg, rs, kh, axis=2)
        s_win = jnp.einsum('bhqd,bhikd->bhqik', q_blk, k_blk).astype(jnp.float32) * scale
        dr = rs + jnp.arange(kh) - r + (WIN_H - 1)
        bias = rpb32[:, dr[None, :, None], dc[:, None, :]]
        s_win = jnp.where(col_in[:, None, :], s_win + bias, NEG_INF)
        s_ctx = jnp.einsum('bhqd,bhld->bhql', q_blk, k_ctx).astype(jnp.float32) * scale
        p = jax.nn.softmax(jnp.concatenate([s_win.reshape(b, h, GRID_W, kh * GRID_W), s_ctx], axis=-1), axis=-1)
        p = p.astype(v.dtype)
        o = jnp.einsum('bhqn,bhnd->bhqd', p[..., :kh * GRID_W], v_blk.reshape(b, h, kh * GRID_W, dh))
        return o + jnp.einsum('bhql,bhld->bhqd', p[..., kh * GRID_W:], v_ctx)

    out = lax.map(row_block, jnp.arange(rows))
    return out.transpose(1, 0, 3, 2, 4).reshape(b, n, h * dh)


def context_attention(q, k, v):
    s = jnp.einsum('bhqd,bhkd->bhqk', q, k).astype(jnp.float32) * (q.shape[-1] ** -0.5)
    p = jax.nn.softmax(s, axis=-1).astype(v.dtype)
    return from_heads(jnp.einsum('bhqk,bhkd->bhqd', p, v))


def peer(h, w_pq, sub_keys, u_tab, v_tab):
    b, n, d = h.shape
    tok = h.reshape(-1, PEER_BLOCK, d)

    def block(hb):
        nb = hb.shape[0]
        q = (hb @ w_pq).reshape(nb, PEER_HEADS, 2, PEER_QDIM // 2)
        s = jnp.einsum('nhpd,hpkd->nhpk', q, sub_keys).astype(jnp.float32)
        s_top, i_top = lax.top_k(s, PEER_TOPK)
        cand = s_top[:, :, 0, :, None] + s_top[:, :, 1, None, :]
        cand_idx = i_top[:, :, 0, :, None] * N_KEYS + i_top[:, :, 1, None, :]
        best_s, best_pos = lax.top_k(cand.reshape(nb, PEER_HEADS, PEER_TOPK * PEER_TOPK), PEER_TOPK)
        expert = jnp.take_along_axis(cand_idx.reshape(nb, PEER_HEADS, PEER_TOPK * PEER_TOPK), best_pos, axis=-1)
        gate = jax.nn.softmax(best_s, axis=-1).astype(hb.dtype).reshape(nb, PEER_HEADS * PEER_TOPK)
        expert = expert.reshape(nb, PEER_HEADS * PEER_TOPK)
        u = u_tab[expert]
        v = v_tab[expert]
        act = jax.nn.gelu(jnp.einsum('nd,ned->ne', hb, u), approximate=False) * gate
        return jnp.einsum('ne,ned->nd', act, v)

    return lax.map(block, tok).reshape(b, n, d)


def setup_inputs(seed: int = 0) -> dict:
    key = jax.random.key(seed)
    ks = jax.random.split(key, 20)
    f32 = jnp.float32
    nrm = lambda k, shape, s: jax.random.normal(k, shape, f32) * s
    gain = lambda k, shape: 1.0 + 0.05 * jax.random.normal(k, shape, f32)
    L = DEPTH
    return {
        "x": nrm(ks[0], (BATCH, SEQ, D_MODEL), 1.0),
        "c": nrm(ks[1], (BATCH, D_MODEL), 1.0),
        "ctx": nrm(ks[2], (BATCH, CTX_LEN, D_MODEL), 1.0),
        "c_ctx": nrm(ks[3], (D_MODEL,), 1.0),
        "w_ada": nrm(ks[4], (L, D_MODEL, N_MOD * D_MODEL), D_MODEL ** -0.5),
        "b_ada": nrm(ks[5], (L, N_MOD * D_MODEL), 0.02),
        "g_norm1": gain(ks[6], (L, D_MODEL)),
        "w_in": nrm(ks[7], (L, D_MODEL, 3 * D_ATTN + 3 * D_CONV), D_MODEL ** -0.5),
        "q_norm": gain(ks[8], (L, HEAD_DIM)),
        "k_norm": gain(ks[9], (L, HEAD_DIM)),
        "rpb": nrm(ks[10], (L, NA_HEADS, 2 * WIN_H - 1, 2 * WIN_W - 1), 0.1),
        "w_conv": nrm(ks[11], (L, CONV_W, D_CONV), CONV_W ** -0.5),
        "g_attn_out": gain(ks[12], (L, NA_HEADS, HEAD_DIM)),
        "g_conv_out": gain(ks[13], (L, CONV_GROUPS, D_CONV // CONV_GROUPS)),
        "w_out": nrm(ks[14], (L, D_MIX, D_MODEL), D_MIX ** -0.5),
        "g_norm2": gain(ks[15], (L, D_MODEL)),
        "w_pq": nrm(ks[16], (L, D_MODEL, PEER_HEADS * PEER_QDIM), D_MODEL ** -0.5),
        "sub_keys": nrm(ks[17], (L, PEER_HEADS, 2, N_KEYS, PEER_QDIM // 2), (PEER_QDIM // 2) ** -0.5),
        "u_experts": nrm(ks[18], (L, N_EXPERTS, D_MODEL), D_MODEL ** -0.5),
        "v_experts": nrm(ks[19], (L, N_EXPERTS, D_MODEL), PEER_HEADS ** -0.5),
    }


def reference(x, c, ctx, c_ctx, w_ada, b_ada, g_norm1, w_in, q_norm, k_norm, rpb, w_conv,
              g_attn_out, g_conv_out, w_out, g_norm2, w_pq, sub_keys, u_experts, v_experts):
    cond = jax.nn.silu(c)
    cond_ctx = jax.nn.silu(c_ctx)
    for l in range(DEPTH):
        update_ctx = l < DEPTH - 1
        mod = (cond @ w_ada[l] + b_ada[l])[:, None, :]
        mod_c = (cond_ctx @ w_ada[l] + b_ada[l])[None, None, :]
        sh1, sc1, gt1, sh2, sc2, gt2 = jnp.split(mod, N_MOD, axis=-1)
        sh1c, sc1c, gt1c, sh2c, sc2c, gt2c = jnp.split(mod_c, N_MOD, axis=-1)

        h = modulate(rms_norm(x, g_norm1[l]), sh1, sc1)
        q, k, v, bg, cg, u = split_proj(h @ w_in[l])
        q = rms_norm(to_heads(q), q_norm[l])
        k = rms_norm(to_heads(k), k_norm[l])
        v = to_heads(v)

        hc = modulate(rms_norm(ctx, g_norm1[l]), sh1c, sc1c)
        if update_ctx:
            qc, kc, vc, bgc, cgc, uc = split_proj(hc @ w_in[l])
        else:
            kc, vc = jnp.split(hc @ w_in[l][:, D_ATTN:3 * D_ATTN], 2, axis=-1)
        kc = rms_norm(to_heads(kc), k_norm[l])
        vc = to_heads(vc)

        o_attn = neighbourhood_attention(q, k, v, kc, vc, rpb[l])
        o_conv = short_conv(u, bg, cg, w_conv[l])
        y = jnp.concatenate([group_rms(o_attn, g_attn_out[l]), group_rms(o_conv, g_conv_out[l])], axis=-1) @ w_out[l]

        if update_ctx:
            oc_attn = context_attention(rms_norm(to_heads(qc), q_norm[l]), kc, vc)
            oc_conv = short_conv(uc, bgc, cgc, w_conv[l])
            yc = jnp.concatenate([group_rms(oc_attn, g_attn_out[l]), group_rms(oc_conv, g_conv_out[l])], axis=-1) @ w_out[l]
            ctx = ctx + gt1c * yc
            h2c = modulate(rms_norm(ctx, g_norm2[l]), sh2c, sc2c)
            ctx = ctx + gt2c * peer(h2c, w_pq[l], sub_keys[l], u_experts[l], v_experts[l])

        x = x + gt1 * y
        h2 = modulate(rms_norm(x, g_norm2[l]), sh2, sc2)
        x = x + gt2 * peer(h2, w_pq[l], sub_keys[l], u_experts[l], v_experts[l])
    return x
```

```python
import functools

import numpy as np
import jax
import jax.numpy as jnp
from jax import lax
from jax.experimental import pallas as pl
from jax.experimental.pallas import tpu as pltpu

HEAD_DIM = 128
GRID_W = 64
WIN_H = 8
WIN_W = 16
CONV_W = 3
PEER_TOPK = 16
EPS = 1e-6
NEG_INF = -1e30

V7X_VMEM_BYTES = 64 * 1024 * 1024
VMEM_LIMIT_BYTES = V7X_VMEM_BYTES - 8 * 1024 * 1024
LANES = 128

ATTN_Q_ROWS = 4
ATTN_K_ROWS = ATTN_Q_ROWS + WIN_H

PEER_GROUP = 8
PEER_STEP = 128

f32 = jnp.float32
bf16 = jnp.bfloat16


def _params(n_grid_dims):
    return pltpu.CompilerParams(
        dimension_semantics=("arbitrary",) * n_grid_dims,
        vmem_limit_bytes=VMEM_LIMIT_BYTES,
    )


def _rms(x):
    return x * lax.rsqrt(jnp.mean(x * x, axis=-1, keepdims=True) + EPS)


def _resident(shape, index_map):
    return pl.BlockSpec(shape, index_map, pipeline_mode=pl.Buffered(1))


def _ada_kernel(c_ref, w_ref, b_ref, o_ref):
    c = c_ref[...]
    cond = (c * jax.nn.sigmoid(c)).astype(bf16)
    o_ref[...] = jnp.dot(cond, w_ref[...].astype(bf16), preferred_element_type=f32) + b_ref[...]


def _ada(cvec, w, b):
    r, d = cvec.shape
    n = w.shape[1]
    tn = 1024
    assert n % tn == 0
    return pl.pallas_call(
        _ada_kernel,
        grid=(n // tn,),
        in_specs=[
            pl.BlockSpec((r, d), lambda j: (0, 0)),
            pl.BlockSpec((d, tn), lambda j: (0, j)),
            pl.BlockSpec((1, tn), lambda j: (0, j)),
        ],
        out_specs=pl.BlockSpec((r, tn), lambda j: (0, j)),
        out_shape=jax.ShapeDtypeStruct((r, n), f32),
        compiler_params=_params(1),
        name="ada_mod",
    )(cvec, w, b)


def _inproj_kernel(x_ref, sh_ref, sc_ref, g_ref, w_ref, hg_ref, o_ref, *, n_norm_heads, cw):
    x = x_ref[...]
    h = (_rms(x) * g_ref[...]) * (1.0 + sc_ref[...]) + sh_ref[...]
    hb = h.astype(bf16)
    n = o_ref.shape[1]
    for j in range(n // cw):
        acc = jnp.dot(hb, w_ref[:, j * cw:(j + 1) * cw], preferred_element_type=f32)
        for k in range(cw // HEAD_DIM):
            head = (j * cw) // HEAD_DIM + k
            blk = acc[:, k * HEAD_DIM:(k + 1) * HEAD_DIM]
            if head < n_norm_heads:
                blk = _rms(blk) * hg_ref[:, head * HEAD_DIM:(head + 1) * HEAD_DIM]
            c0 = head * HEAD_DIM
            o_ref[:, c0:c0 + HEAD_DIM] = blk.astype(bf16)


def _inproj(x2d, mod3, mod_row, g, w, hg, n_norm_heads, tm):
    m, d = x2d.shape
    n = w.shape[1]
    cw = 512
    assert m % tm == 0 and n % cw == 0
    kern = functools.partial(_inproj_kernel, n_norm_heads=n_norm_heads, cw=cw)
    return pl.pallas_call(
        kern,
        grid=(m // tm,),
        in_specs=[
            pl.BlockSpec((tm, d), lambda i: (i, 0)),
            pl.BlockSpec((None, 1, d), lambda i: (mod_row(i), 0, 0)),
            pl.BlockSpec((None, 1, d), lambda i: (mod_row(i), 0, 1)),
            pl.BlockSpec((1, d), lambda i: (0, 0)),
            _resident((d, n), lambda i: (0, 0)),
            pl.BlockSpec((1, hg.shape[1]), lambda i: (0, 0)),
        ],
        out_specs=pl.BlockSpec((tm, n), lambda i: (i, 0)),
        out_shape=jax.ShapeDtypeStruct((m, n), bf16),
        compiler_params=_params(1),
        name="in_proj",
    )(x2d, mod3, mod3, g, w, hg)


def _attn_bias(rpb, rows):
    kh = min(WIN_H, rows)
    qi = np.arange(ATTN_Q_ROWS * GRID_W)
    kj = np.arange(ATTN_K_ROWS * GRID_W)
    tiles = []
    for r0 in (0, ATTN_Q_ROWS, rows - ATTN_Q_ROWS):
        ws = int(np.clip(r0 - kh // 2, 0, rows - ATTN_K_ROWS))
        r = r0 + qi // GRID_W
        c = qi % GRID_W
        kr = ws + kj // GRID_W
        kc = kj % GRID_W
        rs = np.clip(r - kh // 2, 0, rows - kh)
        cs = np.clip(c - WIN_W // 2, 0, GRID_W - WIN_W)
        ok = ((kr[None, :] >= rs[:, None]) & (kr[None, :] < rs[:, None] + kh)
              & (kc[None, :] >= cs[:, None]) & (kc[None, :] < cs[:, None] + WIN_W))
        dr = np.clip(kr[None, :] - r[:, None] + (WIN_H - 1), 0, 2 * WIN_H - 2)
        dc = np.clip(kc[None, :] - c[:, None], -(WIN_W - 1), WIN_W - 1) + (WIN_W - 1)
        tiles.append(jnp.where(jnp.asarray(ok)[None], rpb.astype(f32)[:, dr, dc], NEG_INF))
    return jnp.stack(tiles, axis=1)


def _attn_kernel(q_ref, k_ref, v_ref, kc_ref, vc_ref, bias_ref, g_ref, o_ref, *, rows):
    grp = pl.program_id(2)
    ws = jnp.clip(ATTN_Q_ROWS * grp - WIN_H // 2, 0, rows - ATTN_K_ROWS)
    start = pl.multiple_of(ws * GRID_W, GRID_W)
    nk = ATTN_K_ROWS * GRID_W
    q = q_ref[...]
    kw = k_ref[pl.ds(start, nk), :]
    vw = v_ref[pl.ds(start, nk), :]
    nt = (((1,), (1,)), ((), ()))
    s_w = lax.dot_general(q, kw, nt, preferred_element_type=f32) + bias_ref[...]
    s_c = lax.dot_general(q, kc_ref[...], nt, preferred_element_type=f32)
    m = jnp.maximum(jnp.max(s_w, axis=-1, keepdims=True), jnp.max(s_c, axis=-1, keepdims=True))
    p_w = jnp.exp(s_w - m)
    p_c = jnp.exp(s_c - m)
    denom = jnp.sum(p_w, axis=-1, keepdims=True) + jnp.sum(p_c, axis=-1, keepdims=True)
    o = (jnp.dot(p_w.astype(bf16), vw, preferred_element_type=f32)
         + jnp.dot(p_c.astype(bf16), vc_ref[...], preferred_element_type=f32))
    o = o / denom
    o_ref[...] = (_rms(o) * g_ref[...]).astype(bf16)


def _attention(proj, kvc, bias, g_out, batch, seq, ctx_len, n_heads):
    rows = seq // GRID_W
    n_grp = rows // ATTN_Q_ROWS
    tq = ATTN_Q_ROWS * GRID_W
    nq_blk = seq // tq

    def cfg(g):
        return jnp.where(g == 0, 0, jnp.where(g == n_grp - 1, 2, 1))

    kern = functools.partial(_attn_kernel, rows=rows)
    return pl.pallas_call(
        kern,
        grid=(batch, n_heads, n_grp),
        in_specs=[
            pl.BlockSpec((tq, HEAD_DIM), lambda b, h, g: (b * nq_blk + g, h)),
            pl.BlockSpec((seq, HEAD_DIM), lambda b, h, g: (b, n_heads + h)),
            pl.BlockSpec((seq, HEAD_DIM), lambda b, h, g: (b, 2 * n_heads + h)),
            pl.BlockSpec((ctx_len, HEAD_DIM), lambda b, h, g: (b, h)),
            pl.BlockSpec((ctx_len, HEAD_DIM), lambda b, h, g: (b, n_heads + h)),
            pl.BlockSpec((None, None, tq, ATTN_K_ROWS * GRID_W), lambda b, h, g: (h, cfg(g), 0, 0)),
            pl.BlockSpec((None, 1, HEAD_DIM), lambda b, h, g: (h, 0, 0)),
        ],
        out_specs=pl.BlockSpec((tq, HEAD_DIM), lambda b, h, g: (b * nq_blk + g, h)),
        out_shape=jax.ShapeDtypeStruct((batch * seq, n_heads * HEAD_DIM), bf16),
        compiler_params=_params(3),
        name="nbr_attention",
    )(proj, proj, proj, kvc, kvc, bias, g_out)


HALO = 16


def _mix_kernel(oa_ref, bg_ref, cg_ref, u_ref, cgp_ref, up_ref, cgn_ref, un_ref, x_ref, wc_ref,
                gco_ref, wo_ref, gt1_ref, g2_ref, sh2_ref, sc2_ref, x1_ref, h2_ref, *, tm, seq):
    t0 = (pl.program_id(0) * tm) % seq
    z = cg_ref[...].astype(f32) * u_ref[...].astype(f32)
    zp = cgp_ref[HALO - 1:HALO, :].astype(f32) * up_ref[HALO - 1:HALO, :].astype(f32)
    zn = cgn_ref[0:1, :].astype(f32) * un_ref[0:1, :].astype(f32)
    zp = jnp.where(t0 > 0, zp, 0.0)
    zn = jnp.where(t0 + tm < seq, zn, 0.0)
    row = lax.broadcasted_iota(jnp.int32, (tm, 1), 0)
    z_m1 = jnp.where(row == 0, zp, pltpu.roll(z, 1, 0))
    z_p1 = jnp.where(row == tm - 1, zn, pltpu.roll(z, tm - 1, 0))
    y = wc_ref[0:1, :] * z_m1 + wc_ref[1:2, :] * z + wc_ref[2:3, :] * z_p1
    oc = bg_ref[...].astype(f32) * y
    pieces = []
    for gi in range(oc.shape[1] // LANES):
        blk = oc[:, gi * LANES:(gi + 1) * LANES]
        pieces.append((_rms(blk) * gco_ref[:, gi * LANES:(gi + 1) * LANES]).astype(bf16))
    ocn = jnp.concatenate(pieces, axis=1)
    da = oa_ref.shape[1]
    y2 = (jnp.dot(oa_ref[...], wo_ref[0:da, :], preferred_element_type=f32)
          + jnp.dot(ocn, wo_ref[da:, :], preferred_element_type=f32))
    x1 = x_ref[...] + gt1_ref[...] * y2
    x1_ref[...] = x1
    h2 = (_rms(x1) * g2_ref[...]) * (1.0 + sc2_ref[...]) + sh2_ref[...]
    h2_ref[...] = h2.astype(bf16)


def _mix(oattn, proj, x2d, mod3, w_conv, g_conv, w_out, g2, seq, tm):
    m, d = x2d.shape
    da = oattn.shape[1]
    dc = w_conv.shape[1]
    assert dc % LANES == 0 and da % dc == 0 and tm % HALO == 0 and seq % tm == 0
    cb = (3 * da) // dc
    hb = tm // HALO
    n_halo = m // HALO

    def brow(i):
        return (i * tm) // seq

    kern = functools.partial(_mix_kernel, tm=tm, seq=seq)
    return pl.pallas_call(
        kern,
        grid=(m // tm,),
        in_specs=[
            pl.BlockSpec((tm, da), lambda i: (i, 0)),
            pl.BlockSpec((tm, dc), lambda i: (i, cb)),
            pl.BlockSpec((tm, dc), lambda i: (i, cb + 1)),
            pl.BlockSpec((tm, dc), lambda i: (i, cb + 2)),
            pl.BlockSpec((HALO, dc), lambda i: (jnp.maximum(i * hb - 1, 0), cb + 1)),
            pl.BlockSpec((HALO, dc), lambda i: (jnp.maximum(i * hb - 1, 0), cb + 2)),
            pl.BlockSpec((HALO, dc), lambda i: (jnp.minimum((i + 1) * hb, n_halo - 1), cb + 1)),
            pl.BlockSpec((HALO, dc), lambda i: (jnp.minimum((i + 1) * hb, n_halo - 1), cb + 2)),
            pl.BlockSpec((tm, d), lambda i: (i, 0)),
            pl.BlockSpec((CONV_W, dc), lambda i: (0, 0)),
            pl.BlockSpec((1, dc), lambda i: (0, 0)),
            _resident((da + dc, d), lambda i: (0, 0)),
            pl.BlockSpec((None, 1, d), lambda i: (brow(i), 0, 2)),
            pl.BlockSpec((1, d), lambda i: (0, 0)),
            pl.BlockSpec((None, 1, d), lambda i: (brow(i), 0, 3)),
            pl.BlockSpec((None, 1, d), lambda i: (brow(i), 0, 4)),
        ],
        out_specs=[
            pl.BlockSpec((tm, d), lambda i: (i, 0)),
            pl.BlockSpec((tm, d), lambda i: (i, 0)),
        ],
        out_shape=[jax.ShapeDtypeStruct((m, d), f32), jax.ShapeDtypeStruct((m, d), bf16)],
        compiler_params=_params(1),
        name="conv_out_proj",
    )(oattn, proj, proj, proj, proj, proj, proj, proj, x2d, w_conv, g_conv, w_out,
      mod3, g2, mod3, mod3)


def _cand_layout():
    pos = [k2 for k2 in range(PEER_TOPK)]
    for k1 in range(1, 8):
        pos += [k1 * PEER_TOPK + k2 for k2 in range(8)]
    pos += [k1 * PEER_TOPK for k1 in range(8, PEER_TOPK)]
    return np.asarray(pos, np.float32)


N_CAND = 16 + 7 * 8 + 8


def _top16_desc(s, iota_k, iota_t):
    n_keys = s.shape[0]

    def body(t, carry):
        s, vals, idxs = carry
        m = jnp.max(s, axis=0, keepdims=True)
        idx = jnp.min(jnp.where(s == m, iota_k, float(n_keys)), axis=0, keepdims=True)
        s = jnp.where(iota_k == idx, -jnp.inf, s)
        hit = iota_t == t
        return s, jnp.where(hit, m, vals), jnp.where(hit, idx, idxs)

    zeros = jnp.zeros((PEER_TOPK, s.shape[1]), f32)
    _, vals, idxs = lax.fori_loop(0, PEER_TOPK, body, (s, zeros, zeros))
    return vals, idxs


def _route_kernel(h2_ref, wq_ref, sk_ref, cpos_ref, idx_ref, gate_ref, q_scr, e_scr, g_scr,
                  *, n_heads, tm):
    n_keys = sk_ref.shape[1]
    qf = jnp.dot(h2_ref[...], wq_ref[...], preferred_element_type=f32).astype(bf16)
    for hp in range(2 * n_heads):
        q_scr[hp] = qf[:, hp * HEAD_DIM:(hp + 1) * HEAD_DIM]

    iota_k = lax.broadcasted_iota(jnp.int32, (n_keys, LANES), 0).astype(f32)
    iota_t = lax.broadcasted_iota(jnp.int32, (PEER_TOPK, LANES), 0)
    cpos = cpos_ref[...]
    nt = (((1,), (1,)), ((), ()))

    for lb in range(tm // LANES):
        def head_body(h, _):
            qa = q_scr[2 * h, lb * LANES:(lb + 1) * LANES, :]
            qb = q_scr[2 * h + 1, lb * LANES:(lb + 1) * LANES, :]
            s1 = lax.dot_general(sk_ref[2 * h], qa, nt, preferred_element_type=f32)
            s2 = lax.dot_general(sk_ref[2 * h + 1], qb, nt, preferred_element_type=f32)
            a1, i1 = _top16_desc(s1, iota_k, iota_t)
            a2, i2 = _top16_desc(s2, iota_k, iota_t)
            pieces = [a1[0:1, :] + a2]
            for k1 in range(1, 8):
                pieces.append(a1[k1:k1 + 1, :] + a2[0:8, :])
            pieces.append(a1[8:16, :] + a2[0:1, :])
            cand = jnp.concatenate(pieces, axis=0)

            def pick(t, carry):
                cand, best, bpos = carry
                m = jnp.max(cand, axis=0, keepdims=True)
                p = jnp.min(jnp.where(cand == m, cpos, 1e9), axis=0, keepdims=True)
                cand = jnp.where(cpos == p, -jnp.inf, cand)
                hit = iota_t == t
                return cand, jnp.where(hit, m, best), jnp.where(hit, p, bpos)

            zeros = jnp.zeros((PEER_TOPK, LANES), f32)
            _, best, bpos = lax.fori_loop(0, PEER_TOPK, pick, (cand, zeros, zeros))
            k1f = jnp.floor(bpos * (1.0 / PEER_TOPK))
            k2f = bpos - PEER_TOPK * k1f
            e1 = jnp.zeros_like(bpos)
            e2 = jnp.zeros_like(bpos)
            for j in range(PEER_TOPK):
                e1 = e1 + jnp.where(k1f == float(j), i1[j:j + 1, :], 0.0)
                e2 = e2 + jnp.where(k2f == float(j), i2[j:j + 1, :], 0.0)
            expert = e1 * float(n_keys) + e2
            ex = jnp.exp(best - best[0:1, :])
            gate = ex / jnp.sum(ex, axis=0, keepdims=True)
            r0 = pl.multiple_of(h * PEER_TOPK, PEER_TOPK)
            e_scr[pl.ds(r0, PEER_TOPK), :] = expert
            g_scr[pl.ds(r0, PEER_TOPK), :] = gate
            return 0

        lax.fori_loop(0, n_heads, head_body, 0)
        idx_ref[lb * LANES:(lb + 1) * LANES, :] = e_scr[...].T.astype(jnp.int32)
        gate_ref[:, lb * LANES:(lb + 1) * LANES] = g_scr[...]


def _route(h2, w_pq, sub_keys2, n_heads, tm):
    m, d = h2.shape
    nq = w_pq.shape[1]
    n_keys = sub_keys2.shape[1]
    ne = n_heads * PEER_TOPK
    assert ne == LANES and tm % LANES == 0 and m % tm == 0
    cpos = jnp.asarray(np.broadcast_to(_cand_layout()[:, None], (N_CAND, LANES)).copy())
    kern = functools.partial(_route_kernel, n_heads=n_heads, tm=tm)
    return pl.pallas_call(
        kern,
        grid=(m // tm,),
        in_specs=[
            pl.BlockSpec((tm, d), lambda i: (i, 0)),
            _resident((d, nq), lambda i: (0, 0)),
            _resident((2 * n_heads, n_keys, HEAD_DIM), lambda i: (0, 0, 0)),
            pl.BlockSpec((N_CAND, LANES), lambda i: (0, 0)),
        ],
        out_specs=[
            pl.BlockSpec((tm, ne), lambda i: (i, 0)),
            pl.BlockSpec((ne, tm), lambda i: (0, i)),
        ],
        out_shape=[jax.ShapeDtypeStruct((m, ne), jnp.int32), jax.ShapeDtypeStruct((ne, m), f32)],
        scratch_shapes=[
            pltpu.VMEM((2 * n_heads, tm, HEAD_DIM), bf16),
            pltpu.VMEM((ne, LANES), f32),
            pltpu.VMEM((ne, LANES), f32),
        ],
        compiler_params=_params(1),
        name="peer_route",
    )(h2, w_pq, sub_keys2, cpos)


def _expert_kernel(idx_hbm, uv_hbm, gate_ref, h2_ref, x1_ref, gt2_ref, o_ref,
                   idx_smem, buf, hf_scr, idx_sem, row_sem, *, d, ne):
    step = pl.program_id(0)
    n_groups = PEER_STEP // PEER_GROUP
    rows_per_group = PEER_GROUP * ne
    lane = lax.broadcasted_iota(jnp.int32, (ne, PEER_STEP), 1)

    def idx_copy(grp, slot):
        tok0 = step * PEER_STEP + grp * PEER_GROUP
        return pltpu.make_async_copy(idx_hbm.at[pl.ds(tok0, PEER_GROUP)], idx_smem.at[slot],
                                     idx_sem.at[slot])

    def issue_rows(slot):
        def body(e, _):
            for tok in range(PEER_GROUP):
                row = idx_smem[slot, tok, e]
                pltpu.make_async_copy(uv_hbm.at[pl.ds(row, 1)],
                                      buf.at[slot, pl.ds(tok * ne + e, 1)],
                                      row_sem.at[slot]).start()
            return 0
        lax.fori_loop(0, ne, body, 0)

    def wait_rows(slot):
        pltpu.make_async_copy(uv_hbm.at[pl.ds(0, rows_per_group)], buf.at[slot],
                              row_sem.at[slot]).wait()

    hf_scr[...] = h2_ref[...].astype(f32)
    idx_copy(0, 0).start()
    idx_copy(0, 0).wait()
    issue_rows(0)
    idx_copy(1, 1).start()

    def group_body(grp, _):
        slot = grp % 2
        nxt = 1 - slot

        @pl.when(grp + 1 < n_groups)
        def _():
            idx_copy(grp + 1, nxt).wait()
            issue_rows(nxt)

        @pl.when(grp + 2 < n_groups)
        def _():
            idx_copy(grp + 2, slot).start()

        wait_rows(slot)
        g0 = pl.multiple_of(grp * PEER_GROUP, PEER_GROUP)
        hh = hf_scr[pl.ds(g0, PEER_GROUP), :]
        ys = []
        for tok in range(PEER_GROUP):
            u = buf[slot, tok * ne:(tok + 1) * ne, 0:d]
            v = buf[slot, tok * ne:(tok + 1) * ne, d:2 * d]
            s = jnp.sum(u * hh[tok:tok + 1, :], axis=1, keepdims=True)
            gcol = jnp.sum(jnp.where(lane == g0 + tok, gate_ref[...], 0.0), axis=1, keepdims=True)
            act = 0.5 * s * (1.0 + lax.erf(s * (2.0 ** -0.5))) * gcol
            ys.append(jnp.sum(act * v, axis=0, keepdims=True))
        y = jnp.concatenate(ys, axis=0)
        o_ref[pl.ds(g0, PEER_GROUP), :] = x1_ref[pl.ds(g0, PEER_GROUP), :] + gt2_ref[...] * y
        return 0

    lax.fori_loop(0, n_groups, group_body, 0)


def _experts(idx, gate_t, h2, x1, mod3, uv, seq):
    m, d = h2.shape
    ne = idx.shape[1]
    assert m % PEER_STEP == 0 and seq % PEER_STEP == 0 and PEER_STEP % PEER_GROUP == 0
    kern = functools.partial(_expert_kernel, d=d, ne=ne)
    return pl.pallas_call(
        kern,
        grid=(m // PEER_STEP,),
        in_specs=[
            pl.BlockSpec(memory_space=pl.ANY),
            pl.BlockSpec(memory_space=pl.ANY),
            pl.BlockSpec((ne, PEER_STEP), lambda i: (0, i)),
            pl.BlockSpec((PEER_STEP, d), lambda i: (i, 0)),
            pl.BlockSpec((PEER_STEP, d), lambda i: (i, 0)),
            pl.BlockSpec((None, 1, d), lambda i: ((i * PEER_STEP) // seq, 0, 5)),
        ],
        out_specs=pl.BlockSpec((PEER_STEP, d), lambda i: (i, 0)),
        out_shape=jax.ShapeDtypeStruct((m, d), f32),
        scratch_shapes=[
            pltpu.SMEM((2, PEER_GROUP, ne), jnp.int32),
            pltpu.VMEM((2, PEER_GROUP * ne, 2 * d), f32),
            pltpu.VMEM((PEER_STEP, d), f32),
            pltpu.SemaphoreType.DMA((2,)),
            pltpu.SemaphoreType.DMA((2,)),
        ],
        compiler_params=_params(1),
        name="peer_experts",
    )(idx, uv, gate_t, h2, x1, mod3)


def kernel(x, c, ctx, c_ctx, w_ada, b_ada, g_norm1, w_in, q_norm, k_norm, rpb, w_conv,
           g_attn_out, g_conv_out, w_out, g_norm2, w_pq, sub_keys, u_experts, v_experts):
    batch, seq, d = x.shape
    ctx_len = ctx.shape[1]
    depth = w_ada.shape[0]
    n_heads = rpb.shape[1]
    d_attn = n_heads * HEAD_DIM
    d_conv = w_conv.shape[2]
    p_heads = sub_keys.shape[1]
    assert d % LANES == 0 and seq % (ATTN_Q_ROWS * GRID_W) == 0 and seq // GRID_W >= ATTN_K_ROWS
    assert w_in.shape[2] == 3 * d_attn + 3 * d_conv and g_conv_out.shape[2] == LANES
    assert sub_keys.shape[2] == 2 and sub_keys.shape[4] == HEAD_DIM
    tm = 256

    x2d = x.reshape(batch * seq, d)
    ctx2d = ctx.reshape(batch * ctx_len, d)
    mod_rows = -(-(batch + 1) // 8) * 8
    cvec = jnp.zeros((mod_rows, d), f32).at[:batch].set(c).at[batch].set(c_ctx)

    for l in range(depth):
        assert l == depth - 1
        mod = _ada(cvec, w_ada[l], b_ada[l][None, :])
        mod3 = mod.reshape(mod_rows, 1, mod.shape[1])

        w_in_b = w_in[l].astype(bf16)
        g1 = g_norm1[l][None, :]
        hg = jnp.concatenate([jnp.tile(q_norm[l], n_heads) * (HEAD_DIM ** -0.5),
                              jnp.tile(k_norm[l], n_heads)])[None, :]
        proj = _inproj(x2d, mod3, lambda i: (i * tm) // seq, g1, w_in_b, hg, 2 * n_heads, tm)
        hgc = jnp.tile(k_norm[l], n_heads)[None, :]
        kvc = _inproj(ctx2d, mod3, lambda i: batch, g1, w_in_b[:, d_attn:3 * d_attn], hgc,
                      n_heads, tm)

        bias = _attn_bias(rpb[l], seq // GRID_W)
        oattn = _attention(proj, kvc, bias, g_attn_out[l][:, None, :], batch, seq, ctx_len, n_heads)

        x1, h2 = _mix(oattn, proj, x2d, mod3, w_conv[l], g_conv_out[l].reshape(1, d_conv),
                      w_out[l].astype(bf16), g_norm2[l][None, :], seq, tm)

        sk2 = sub_keys[l].reshape(2 * p_heads, sub_keys.shape[3], HEAD_DIM).astype(bf16)
        idx, gate_t = _route(h2, w_pq[l].astype(bf16), sk2, p_heads, tm)

        uv = jnp.concatenate([u_experts[l], v_experts[l]], axis=1)
        x2d = _experts(idx, gate_t, h2, x1, mod3, uv, seq)
    return x2d.reshape(batch, seq, d)
```

```python
import functools

import numpy as np
import jax
import jax.numpy as jnp
from jax import lax
from jax.experimental import pallas as pl
from jax.experimental.pallas import tpu as pltpu

HEAD_DIM = 128
GRID_W = 64
WIN_H = 8
WIN_W = 16
CONV_W = 3
PEER_TOPK = 16
EPS = 1e-6
NEG_INF = -1e30

V7X_VMEM_BYTES = 64 * 1024 * 1024
VMEM_LIMIT_BYTES = V7X_VMEM_BYTES - 8 * 1024 * 1024
LANES = 128

ATTN_Q_ROWS = 4
ATTN_K_ROWS = ATTN_Q_ROWS + WIN_H

PEER_GROUP = 8
PEER_STEP = 128
DOT_COPIES = 72

f32 = jnp.float32
bf16 = jnp.bfloat16


def _params(n_grid_dims):
    return pltpu.CompilerParams(
        dimension_semantics=("arbitrary",) * n_grid_dims,
        vmem_limit_bytes=VMEM_LIMIT_BYTES,
    )


def _rms(x):
    return x * lax.rsqrt(jnp.mean(x * x, axis=-1, keepdims=True) + EPS)


def _resident(shape, index_map):
    return pl.BlockSpec(shape, index_map, pipeline_mode=pl.Buffered(1))


def _ada_kernel(c_ref, w_ref, b_ref, o_ref):
    c = c_ref[...]
    cond = (c * jax.nn.sigmoid(c)).astype(bf16)
    o_ref[...] = jnp.dot(cond, w_ref[...].astype(bf16), preferred_element_type=f32) + b_ref[...]


def _ada(cvec, w, b):
    r, d = cvec.shape
    n = w.shape[1]
    tn = 1024
    assert n % tn == 0
    return pl.pallas_call(
        _ada_kernel,
        grid=(n // tn,),
        in_specs=[
            pl.BlockSpec((r, d), lambda j: (0, 0)),
            pl.BlockSpec((d, tn), lambda j: (0, j)),
            pl.BlockSpec((1, tn), lambda j: (0, j)),
        ],
        out_specs=pl.BlockSpec((r, tn), lambda j: (0, j)),
        out_shape=jax.ShapeDtypeStruct((r, n), f32),
        compiler_params=_params(1),
        name="ada_mod",
    )(cvec, w, b)


def _inproj_kernel(x_ref, sh_ref, sc_ref, g_ref, w_ref, hg_ref, o_ref, *, n_norm_heads, cw):
    x = x_ref[...]
    h = (_rms(x) * g_ref[...]) * (1.0 + sc_ref[...]) + sh_ref[...]
    hb = h.astype(bf16)
    n = o_ref.shape[1]
    for j in range(n // cw):
        acc = jnp.dot(hb, w_ref[:, j * cw:(j + 1) * cw], preferred_element_type=f32)
        for k in range(cw // HEAD_DIM):
            head = (j * cw) // HEAD_DIM + k
            blk = acc[:, k * HEAD_DIM:(k + 1) * HEAD_DIM]
            if head < n_norm_heads:
                blk = _rms(blk) * hg_ref[:, head * HEAD_DIM:(head + 1) * HEAD_DIM]
            c0 = head * HEAD_DIM
            o_ref[:, c0:c0 + HEAD_DIM] = blk.astype(bf16)


def _inproj(x2d, mod3, mod_row, g, w, hg, n_norm_heads, tm):
    m, d = x2d.shape
    n = w.shape[1]
    cw = 512
    assert m % tm == 0 and n % cw == 0
    kern = functools.partial(_inproj_kernel, n_norm_heads=n_norm_heads, cw=cw)
    return pl.pallas_call(
        kern,
        grid=(m // tm,),
        in_specs=[
            pl.BlockSpec((tm, d), lambda i: (i, 0)),
            pl.BlockSpec((None, 1, d), lambda i: (mod_row(i), 0, 0)),
            pl.BlockSpec((None, 1, d), lambda i: (mod_row(i), 0, 1)),
            pl.BlockSpec((1, d), lambda i: (0, 0)),
            _resident((d, n), lambda i: (0, 0)),
            pl.BlockSpec((1, hg.shape[1]), lambda i: (0, 0)),
        ],
        out_specs=pl.BlockSpec((tm, n), lambda i: (i, 0)),
        out_shape=jax.ShapeDtypeStruct((m, n), bf16),
        compiler_params=_params(1),
        name="in_proj",
    )(x2d, mod3, mod3, g, w, hg)


def _attn_bias(rpb, rows):
    kh = min(WIN_H, rows)
    n_heads = rpb.shape[0]
    col = np.arange(GRID_W)
    cs = np.clip(col - WIN_W // 2, 0, GRID_W - WIN_W)
    col_ok = (col[None, :] >= cs[:, None]) & (col[None, :] < cs[:, None] + WIN_W)
    dc = np.clip(col[None, :] - col[:, None], -(WIN_W - 1), WIN_W - 1) + (WIN_W - 1)
    onehot = dc[None] == np.arange(2 * WIN_W - 1)[:, None, None]
    by_dr = jnp.sum(jnp.where(onehot[None, None], rpb.astype(f32)[:, :, :, None, None], 0.0), axis=2)
    by_dr = jnp.where(col_ok[None, None], by_dr, NEG_INF)
    masked = jnp.full((n_heads, GRID_W, GRID_W), NEG_INF, f32)
    tiles = []
    for r0 in (0, ATTN_Q_ROWS, rows - ATTN_Q_ROWS):
        ws = int(np.clip(r0 - kh // 2, 0, rows - ATTN_K_ROWS))
        q_rows = []
        for r in range(r0, r0 + ATTN_Q_ROWS):
            rs = int(np.clip(r - kh // 2, 0, rows - kh))
            blocks = [by_dr[:, kr - r + (WIN_H - 1)] if rs <= kr < rs + kh else masked
                      for kr in range(ws, ws + ATTN_K_ROWS)]
            q_rows.append(jnp.concatenate(blocks, axis=2))
        tiles.append(jnp.concatenate(q_rows, axis=1))
    return jnp.stack(tiles, axis=1)


def _attn_kernel(q_ref, k_ref, v_ref, kc_ref, vc_ref, bias_ref, g_ref, o_ref, *, rows):
    grp = pl.program_id(2)
    ws = jnp.clip(ATTN_Q_ROWS * grp - WIN_H // 2, 0, rows - ATTN_K_ROWS)
    start = pl.multiple_of(ws * GRID_W, GRID_W)
    nk = ATTN_K_ROWS * GRID_W
    q = q_ref[...]
    kw = k_ref[pl.ds(start, nk), :]
    vw = v_ref[pl.ds(start, nk), :]
    nt = (((1,), (1,)), ((), ()))
    s_w = lax.dot_general(q, kw, nt, preferred_element_type=f32) + bias_ref[...]
    s_c = lax.dot_general(q, kc_ref[...], nt, preferred_element_type=f32)
    m = jnp.maximum(jnp.max(s_w, axis=-1, keepdims=True), jnp.max(s_c, axis=-1, keepdims=True))
    p_w = jnp.exp(s_w - m)
    p_c = jnp.exp(s_c - m)
    denom = jnp.sum(p_w, axis=-1, keepdims=True) + jnp.sum(p_c, axis=-1, keepdims=True)
    o = (jnp.dot(p_w.astype(bf16), vw, preferred_element_type=f32)
         + jnp.dot(p_c.astype(bf16), vc_ref[...], preferred_element_type=f32))
    o = o / denom
    o_ref[...] = (_rms(o) * g_ref[...]).astype(bf16)


def _attention(proj, kvc, bias, g_out, batch, seq, ctx_len, n_heads):
    rows = seq // GRID_W
    n_grp = rows // ATTN_Q_ROWS
    tq = ATTN_Q_ROWS * GRID_W
    nq_blk = seq // tq

    def cfg(g):
        return jnp.where(g == 0, 0, jnp.where(g == n_grp - 1, 2, 1))

    kern = functools.partial(_attn_kernel, rows=rows)
    return pl.pallas_call(
        kern,
        grid=(batch, n_heads, n_grp),
        in_specs=[
            pl.BlockSpec((tq, HEAD_DIM), lambda b, h, g: (b * nq_blk + g, h)),
            pl.BlockSpec((seq, HEAD_DIM), lambda b, h, g: (b, n_heads + h)),
            pl.BlockSpec((seq, HEAD_DIM), lambda b, h, g: (b, 2 * n_heads + h)),
            pl.BlockSpec((ctx_len, HEAD_DIM), lambda b, h, g: (b, h)),
            pl.BlockSpec((ctx_len, HEAD_DIM), lambda b, h, g: (b, n_heads + h)),
            pl.BlockSpec((None, None, tq, ATTN_K_ROWS * GRID_W), lambda b, h, g: (h, cfg(g), 0, 0)),
            pl.BlockSpec((None, 1, HEAD_DIM), lambda b, h, g: (h, 0, 0)),
        ],
        out_specs=pl.BlockSpec((tq, HEAD_DIM), lambda b, h, g: (b * nq_blk + g, h)),
        out_shape=jax.ShapeDtypeStruct((batch * seq, n_heads * HEAD_DIM), bf16),
        compiler_params=_params(3),
        name="nbr_attention",
    )(proj, proj, proj, kvc, kvc, bias, g_out)


HALO = 16


def _mix_kernel(oa_ref, bg_ref, cg_ref, u_ref, cgp_ref, up_ref, cgn_ref, un_ref, x_ref, wc_ref,
                gco_ref, wo_ref, gt1_ref, g2_ref, sh2_ref, sc2_ref, x1_ref, h2_ref, *, tm, seq):
    t0 = (pl.program_id(0) * tm) % seq
    z = cg_ref[...].astype(f32) * u_ref[...].astype(f32)
    zp = cgp_ref[HALO - 1:HALO, :].astype(f32) * up_ref[HALO - 1:HALO, :].astype(f32)
    zn = cgn_ref[0:1, :].astype(f32) * un_ref[0:1, :].astype(f32)
    zp = jnp.where(t0 > 0, zp, 0.0)
    zn = jnp.where(t0 + tm < seq, zn, 0.0)
    row = lax.broadcasted_iota(jnp.int32, (tm, 1), 0)
    z_m1 = jnp.where(row == 0, zp, pltpu.roll(z, 1, 0))
    z_p1 = jnp.where(row == tm - 1, zn, pltpu.roll(z, tm - 1, 0))
    y = wc_ref[0:1, :] * z_m1 + wc_ref[1:2, :] * z + wc_ref[2:3, :] * z_p1
    oc = bg_ref[...].astype(f32) * y
    pieces = []
    for gi in range(oc.shape[1] // LANES):
        blk = oc[:, gi * LANES:(gi + 1) * LANES]
        pieces.append((_rms(blk) * gco_ref[:, gi * LANES:(gi + 1) * LANES]).astype(bf16))
    ocn = jnp.concatenate(pieces, axis=1)
    da = oa_ref.shape[1]
    y2 = (jnp.dot(oa_ref[...], wo_ref[0:da, :], preferred_element_type=f32)
          + jnp.dot(ocn, wo_ref[da:, :], preferred_element_type=f32))
    x1 = x_ref[...] + gt1_ref[...] * y2
    x1_ref[...] = x1
    h2 = (_rms(x1) * g2_ref[...]) * (1.0 + sc2_ref[...]) + sh2_ref[...]
    h2_ref[...] = h2.astype(bf16)


def _mix(oattn, proj, x2d, mod3, w_conv, g_conv, w_out, g2, seq, tm):
    m, d = x2d.shape
    da = oattn.shape[1]
    dc = w_conv.shape[1]
    assert dc % LANES == 0 and da % dc == 0 and tm % HALO == 0 and seq % tm == 0
    cb = (3 * da) // dc
    hb = tm // HALO
    n_halo = m // HALO

    def brow(i):
        return (i * tm) // seq

    kern = functools.partial(_mix_kernel, tm=tm, seq=seq)
    return pl.pallas_call(
        kern,
        grid=(m // tm,),
        in_specs=[
            pl.BlockSpec((tm, da), lambda i: (i, 0)),
            pl.BlockSpec((tm, dc), lambda i: (i, cb)),
            pl.BlockSpec((tm, dc), lambda i: (i, cb + 1)),
            pl.BlockSpec((tm, dc), lambda i: (i, cb + 2)),
            pl.BlockSpec((HALO, dc), lambda i: (jnp.maximum(i * hb - 1, 0), cb + 1)),
            pl.BlockSpec((HALO, dc), lambda i: (jnp.maximum(i * hb - 1, 0), cb + 2)),
            pl.BlockSpec((HALO, dc), lambda i: (jnp.minimum((i + 1) * hb, n_halo - 1), cb + 1)),
            pl.BlockSpec((HALO, dc), lambda i: (jnp.minimum((i + 1) * hb, n_halo - 1), cb + 2)),
            pl.BlockSpec((tm, d), lambda i: (i, 0)),
            pl.BlockSpec((CONV_W, dc), lambda i: (0, 0)),
            pl.BlockSpec((1, dc), lambda i: (0, 0)),
            _resident((da + dc, d), lambda i: (0, 0)),
            pl.BlockSpec((None, 1, d), lambda i: (brow(i), 0, 2)),
            pl.BlockSpec((1, d), lambda i: (0, 0)),
            pl.BlockSpec((None, 1, d), lambda i: (brow(i), 0, 3)),
            pl.BlockSpec((None, 1, d), lambda i: (brow(i), 0, 4)),
        ],
        out_specs=[
            pl.BlockSpec((tm, d), lambda i: (i, 0)),
            pl.BlockSpec((tm, d), lambda i: (i, 0)),
        ],
        out_shape=[jax.ShapeDtypeStruct((m, d), f32), jax.ShapeDtypeStruct((m, d), bf16)],
        compiler_params=_params(1),
        name="conv_out_proj",
    )(oattn, proj, proj, proj, proj, proj, proj, proj, x2d, w_conv, g_conv, w_out,
      mod3, g2, mod3, mod3)


def _cand_layout():
    pos = [k2 for k2 in range(PEER_TOPK)]
    for k1 in range(1, 8):
        pos += [k1 * PEER_TOPK + k2 for k2 in range(8)]
    pos += [k1 * PEER_TOPK for k1 in range(8, PEER_TOPK)]
    return np.asarray(pos, np.float32)


N_CAND = 16 + 7 * 8 + 8


def _top16_desc(s, iota_k, iota_t):
    n_keys = s.shape[0]

    def body(t, carry):
        s, vals, idxs = carry
        m = jnp.max(s, axis=0, keepdims=True)
        idx = jnp.min(jnp.where(s == m, iota_k, float(n_keys)), axis=0, keepdims=True)
        s = jnp.where(iota_k == idx, -jnp.inf, s)
        hit = iota_t == t
        return s, jnp.where(hit, m, vals), jnp.where(hit, idx, idxs)

    zeros = jnp.zeros((PEER_TOPK, s.shape[1]), f32)
    _, vals, idxs = lax.fori_loop(0, PEER_TOPK, body, (s, zeros, zeros))
    return vals, idxs


def _route_kernel(h2_ref, wq_ref, sk_ref, cpos_ref, idx_ref, gate_ref, q_scr, e_scr, g_scr,
                  *, n_heads, tm):
    n_keys = sk_ref.shape[1]
    qf = jnp.dot(h2_ref[...], wq_ref[...], preferred_element_type=f32).astype(bf16)
    for hp in range(2 * n_heads):
        q_scr[hp] = qf[:, hp * HEAD_DIM:(hp + 1) * HEAD_DIM]

    iota_k = lax.broadcasted_iota(jnp.int32, (n_keys, 2 * LANES), 0).astype(f32)
    iota_t2 = lax.broadcasted_iota(jnp.int32, (PEER_TOPK, 2 * LANES), 0)
    iota_t = lax.broadcasted_iota(jnp.int32, (PEER_TOPK, LANES), 0)
    cpos = cpos_ref[...]
    nt = (((1,), (1,)), ((), ()))

    for lb in range(tm // LANES):
        def head_body(h, _):
            qa = q_scr[2 * h, lb * LANES:(lb + 1) * LANES, :]
            qb = q_scr[2 * h + 1, lb * LANES:(lb + 1) * LANES, :]
            s1 = lax.dot_general(sk_ref[2 * h], qa, nt, preferred_element_type=f32)
            s2 = lax.dot_general(sk_ref[2 * h + 1], qb, nt, preferred_element_type=f32)
            a12, i12 = _top16_desc(jnp.concatenate([s1, s2], axis=1), iota_k, iota_t2)
            a1, a2 = a12[:, :LANES], a12[:, LANES:]
            i1, i2 = i12[:, :LANES], i12[:, LANES:]
            pieces = [a1[0:1, :] + a2]
            for k1 in range(1, 8):
                pieces.append(a1[k1:k1 + 1, :] + a2[0:8, :])
            pieces.append(a1[8:16, :] + a2[0:1, :])
            cand = jnp.concatenate(pieces, axis=0)

            def pick(t, carry):
                cand, best, bpos = carry
                m = jnp.max(cand, axis=0, keepdims=True)
                p = jnp.min(jnp.where(cand == m, cpos, 1e9), axis=0, keepdims=True)
                cand = jnp.where(cpos == p, -jnp.inf, cand)
                hit = iota_t == t
                return cand, jnp.where(hit, m, best), jnp.where(hit, p, bpos)

            zeros = jnp.zeros((PEER_TOPK, LANES), f32)
            _, best, bpos = lax.fori_loop(0, PEER_TOPK, pick, (cand, zeros, zeros))
            k1f = jnp.floor(bpos * (1.0 / PEER_TOPK))
            k2f = bpos - PEER_TOPK * k1f
            e1 = jnp.zeros_like(bpos)
            e2 = jnp.zeros_like(bpos)
            for j in range(PEER_TOPK):
                e1 = e1 + jnp.where(k1f == float(j), i1[j:j + 1, :], 0.0)
                e2 = e2 + jnp.where(k2f == float(j), i2[j:j + 1, :], 0.0)
            expert = e1 * float(n_keys) + e2
            ex = jnp.exp(best - best[0:1, :])
            gate = ex / jnp.sum(ex, axis=0, keepdims=True)
            r0 = pl.multiple_of(h * PEER_TOPK, PEER_TOPK)
            e_scr[pl.ds(r0, PEER_TOPK), :] = expert
            g_scr[pl.ds(r0, PEER_TOPK), :] = gate
            return 0

        lax.fori_loop(0, n_heads, head_body, 0)
        idx_ref[lb * LANES:(lb + 1) * LANES, :] = e_scr[...].T.astype(jnp.int32)
        gate_ref[:, lb * LANES:(lb + 1) * LANES] = g_scr[...]


def _route(h2, w_pq, sub_keys2, n_heads, tm):
    m, d = h2.shape
    nq = w_pq.shape[1]
    n_keys = sub_keys2.shape[1]
    ne = n_heads * PEER_TOPK
    assert ne == LANES and tm % LANES == 0 and m % tm == 0
    cpos = jnp.asarray(np.broadcast_to(_cand_layout()[:, None], (N_CAND, LANES)).copy())
    kern = functools.partial(_route_kernel, n_heads=n_heads, tm=tm)
    return pl.pallas_call(
        kern,
        grid=(m // tm,),
        in_specs=[
            pl.BlockSpec((tm, d), lambda i: (i, 0)),
            _resident((d, nq), lambda i: (0, 0)),
            _resident((2 * n_heads, n_keys, HEAD_DIM), lambda i: (0, 0, 0)),
            pl.BlockSpec((N_CAND, LANES), lambda i: (0, 0)),
        ],
        out_specs=[
            pl.BlockSpec((tm, ne), lambda i: (i, 0)),
            pl.BlockSpec((ne, tm), lambda i: (0, i)),
        ],
        out_shape=[jax.ShapeDtypeStruct((m, ne), jnp.int32), jax.ShapeDtypeStruct((ne, m), f32)],
        scratch_shapes=[
            pltpu.VMEM((2 * n_heads, tm, HEAD_DIM), bf16),
            pltpu.VMEM((ne, LANES), f32),
            pltpu.VMEM((ne, LANES), f32),
        ],
        compiler_params=_params(1),
        name="peer_route",
    )(h2, w_pq, sub_keys2, cpos)


SLAB = 16


def _sublane_sums(r, sub):
    def merge(a, b, half):
        lo = (sub % (2 * half)) < half
        x = jnp.where(lo, a, pltpu.roll(b, half, 0))
        y = jnp.where(lo, pltpu.roll(a, 8 - half, 0), b)
        return x + y
    a = [merge(r[j], r[j + 4], 4) for j in range(4)]
    b = [merge(a[j], a[j + 2], 2) for j in range(2)]
    return merge(b[0], b[1], 1)


def _expert_kernel(idx_hbm, uv_hbm, gate_ref, h2_ref, x1_ref, gt2_ref, o_ref,
                   idx_smem, buf_even, buf_odd, hs_scr, w_scr, a_scr, ys_scr, idx_sem, row_sem, *, d, ne):
    step = pl.program_id(0)
    n_steps = pl.num_programs(0)
    groups_per_step = PEER_STEP // PEER_GROUP
    last_group = n_steps * groups_per_step - 1
    n_tiles = d // LANES
    lane = lax.broadcasted_iota(jnp.int32, (ne, PEER_STEP), 1)
    sub = lax.broadcasted_iota(jnp.int32, (8, LANES), 0)
    bufs = (buf_even, buf_odd)

    def idx_copy(gg, slot):
        tok0 = jnp.minimum(gg, last_group) * PEER_GROUP
        return pltpu.make_async_copy(idx_hbm.at[pl.ds(tok0, PEER_GROUP)], idx_smem.at[slot],
                                     idx_sem.at[slot])

    def row_copy(slot, tok, e):
        return pltpu.make_async_copy(uv_hbm.at[idx_smem[slot, tok, e]], bufs[slot].at[tok * ne + e],
                                     row_sem.at[slot])

    def wait_rows(slot):
        pltpu.make_async_copy(uv_hbm.at[pl.ds(0, PEER_GROUP * ne)], bufs[slot],
                              row_sem.at[slot]).wait()

    @pl.when(step == 0)
    def _():
        w_scr[...] = jnp.zeros(w_scr.shape, f32)
        idx_copy(0, 0).start()
        idx_copy(0, 0).wait()

        def prime(tok, _):
            for e in range(ne):
                row_copy(0, tok, e).start()
            return 0
        lax.fori_loop(0, PEER_GROUP, prime, 0)
        idx_copy(1, 1).start()

    hf = h2_ref[...].astype(f32)
    for rg in range(PEER_STEP // 8):
        for j in range(n_tiles):
            hs_scr[pl.ds(rg * 8 * n_tiles + j, 8, stride=n_tiles), :] = (
                hf[rg * 8:(rg + 1) * 8, j * LANES:(j + 1) * LANES])

    def do_group(grp, slot):
        gg = step * groups_per_step + grp
        nslot = 1 - slot
        cur = bufs[slot]
        g0 = pl.multiple_of(grp * PEER_GROUP, PEER_GROUP)
        idx_copy(gg + 1, nslot).wait()
        idx_copy(gg + 2, slot).start()
        wait_rows(slot)

        def activate(tok):
            base = pl.multiple_of(tok * ne, ne)
            s = jnp.sum(w_scr[pl.ds(base, ne), :], axis=1, keepdims=True)
            gcol = jnp.sum(jnp.where(lane == g0 + tok, gate_ref[...], 0.0), axis=1, keepdims=True)
            act = 0.5 * s * (1.0 + lax.erf(s * (2.0 ** -0.5))) * gcol
            a_scr[pl.ds(base, ne), :] = jnp.broadcast_to(act, (ne, LANES))

        def dot_body(tok, _):
            base = pl.multiple_of(tok * ne, ne)
            activate((tok + PEER_GROUP - 1) % PEER_GROUP)
            hs = hs_scr[pl.ds(pl.multiple_of((g0 + tok) * n_tiles, n_tiles), n_tiles), :]
            for g in range(ne // 8):
                parts = []
                for j in range(8):
                    p = cur[base + g * 8 + j, 0:SLAB, :].astype(f32) * hs
                    parts.append(p[0:8, :] + p[8:16, :])
                w_scr[pl.ds(base + g * 8, 8), :] = _sublane_sums(parts, sub)
            for e in range(DOT_COPIES):
                row_copy(nslot, tok, e).start()
            return 0

        lax.fori_loop(0, PEER_GROUP, dot_body, 0)
        activate(PEER_GROUP - 1)

        def axpy_body(tok, _):
            base = pl.multiple_of(tok * ne, ne)
            accs = [jnp.zeros((SLAB, LANES), f32) for _ in range(4)]
            for e in range(ne):
                v = cur[base + e, SLAB:2 * SLAB, :].astype(f32)
                accs[e % 4] = accs[e % 4] + a_scr[pl.ds(base + e, 1), :] * v
            for e in range(DOT_COPIES, ne):
                row_copy(nslot, tok, e).start()
            ys_scr[pl.ds(pl.multiple_of(tok * n_tiles, n_tiles), n_tiles), :] = (
                (accs[0] + accs[1]) + (accs[2] + accs[3]))
            return 0

        lax.fori_loop(0, PEER_GROUP, axpy_body, 0)
        y = jnp.concatenate([ys_scr[pl.ds(j, PEER_GROUP, stride=n_tiles), :] for j in range(n_tiles)],
                            axis=1)
        o_ref[pl.ds(g0, PEER_GROUP), :] = x1_ref[pl.ds(g0, PEER_GROUP), :] + gt2_ref[...] * y

    def pair_body(pair, _):
        do_group(2 * pair, 0)
        do_group(2 * pair + 1, 1)
        return 0

    lax.fori_loop(0, groups_per_step // 2, pair_body, 0)

    @pl.when(step == n_steps - 1)
    def _():
        wait_rows(0)
        idx_copy(last_group, 1).wait()


def _experts(idx, gate_t, h2, x1, mod3, uv, seq):
    m, d = h2.shape
    ne = idx.shape[1]
    assert d == SLAB * LANES and uv.shape[1:] == (2 * SLAB, LANES) and ne % 8 == 0
    assert m % PEER_STEP == 0 and seq % PEER_STEP == 0 and PEER_GROUP == 8
    assert (PEER_STEP // PEER_GROUP) % 2 == 0
    kern = functools.partial(_expert_kernel, d=d, ne=ne)
    return pl.pallas_call(
        kern,
        grid=(m // PEER_STEP,),
        in_specs=[
            pl.BlockSpec(memory_space=pl.ANY),
            pl.BlockSpec(memory_space=pl.ANY),
            pl.BlockSpec((ne, PEER_STEP), lambda i: (0, i)),
            pl.BlockSpec((PEER_STEP, d), lambda i: (i, 0)),
            pl.BlockSpec((PEER_STEP, d), lambda i: (i, 0)),
            pl.BlockSpec((None, 1, d), lambda i: ((i * PEER_STEP) // seq, 0, 5)),
        ],
        out_specs=pl.BlockSpec((PEER_STEP, d), lambda i: (i, 0)),
        out_shape=jax.ShapeDtypeStruct((m, d), f32),
        scratch_shapes=[
            pltpu.SMEM((2, PEER_GROUP, ne), jnp.int32),
            pltpu.VMEM((PEER_GROUP * ne, 2 * SLAB, LANES), bf16),
            pltpu.VMEM((PEER_GROUP * ne, 2 * SLAB, LANES), bf16),
            pltpu.VMEM((PEER_STEP * SLAB, LANES), f32),
            pltpu.VMEM((PEER_GROUP * ne, LANES), f32),
            pltpu.VMEM((PEER_GROUP * ne, LANES), f32),
            pltpu.VMEM((PEER_GROUP * SLAB, LANES), f32),
            pltpu.SemaphoreType.DMA((2,)),
            pltpu.SemaphoreType.DMA((2,)),
        ],
        compiler_params=_params(1),
        name="peer_experts",
    )(idx, uv, gate_t, h2, x1, mod3)


def kernel(x, c, ctx, c_ctx, w_ada, b_ada, g_norm1, w_in, q_norm, k_norm, rpb, w_conv,
           g_attn_out, g_conv_out, w_out, g_norm2, w_pq, sub_keys, u_experts, v_experts):
    batch, seq, d = x.shape
    ctx_len = ctx.shape[1]
    depth = w_ada.shape[0]
    n_heads = rpb.shape[1]
    d_attn = n_heads * HEAD_DIM
    d_conv = w_conv.shape[2]
    p_heads = sub_keys.shape[1]
    assert d % LANES == 0 and seq % (ATTN_Q_ROWS * GRID_W) == 0 and seq // GRID_W >= ATTN_K_ROWS
    assert w_in.shape[2] == 3 * d_attn + 3 * d_conv and g_conv_out.shape[2] == LANES
    assert sub_keys.shape[2] == 2 and sub_keys.shape[4] == HEAD_DIM
    tm = 256

    x2d = x.reshape(batch * seq, d)
    ctx2d = ctx.reshape(batch * ctx_len, d)
    mod_rows = -(-(batch + 1) // 8) * 8
    cvec = jnp.zeros((mod_rows, d), f32).at[:batch].set(c).at[batch].set(c_ctx)

    for l in range(depth):
        assert l == depth - 1
        mod = _ada(cvec, w_ada[l], b_ada[l][None, :])
        mod3 = mod.reshape(mod_rows, 1, mod.shape[1])

        w_in_b = w_in[l].astype(bf16)
        g1 = g_norm1[l][None, :]
        hg = jnp.concatenate([jnp.tile(q_norm[l], n_heads) * (HEAD_DIM ** -0.5),
                              jnp.tile(k_norm[l], n_heads)])[None, :]
        proj = _inproj(x2d, mod3, lambda i: (i * tm) // seq, g1, w_in_b, hg, 2 * n_heads, tm)
        hgc = jnp.tile(k_norm[l], n_heads)[None, :]
        kvc = _inproj(ctx2d, mod3, lambda i: batch, g1, w_in_b[:, d_attn:3 * d_attn], hgc,
                      n_heads, tm)

        bias = _attn_bias(rpb[l], seq // GRID_W)
        oattn = _attention(proj, kvc, bias, g_attn_out[l][:, None, :], batch, seq, ctx_len, n_heads)

        x1, h2 = _mix(oattn, proj, x2d, mod3, w_conv[l], g_conv_out[l].reshape(1, d_conv),
                      w_out[l].astype(bf16), g_norm2[l][None, :], seq, tm)

        sk2 = sub_keys[l].reshape(2 * p_heads, sub_keys.shape[3], HEAD_DIM).astype(bf16)
        idx, gate_t = _route(h2, w_pq[l].astype(bf16), sk2, p_heads, tm)

        n_exp = u_experts.shape[1]
        uv = jnp.concatenate([u_experts[l].astype(bf16).reshape(n_exp, SLAB, LANES),
                              v_experts[l].astype(bf16).reshape(n_exp, SLAB, LANES)], axis=1)
        x2d = _experts(idx, gate_t, h2, x1, mod3, uv, seq)
    return x2d.reshape(batch, seq, d)
```

```python
import functools

import numpy as np
import jax
import jax.numpy as jnp
from jax import lax
from jax.experimental import pallas as pl
from jax.experimental.pallas import tpu as pltpu

HEAD_DIM = 128
GRID_W = 64
WIN_H = 8
WIN_W = 16
CONV_W = 3
PEER_TOPK = 16
EPS = 1e-6
NEG_INF = -1e30

V7X_VMEM_BYTES = 64 * 1024 * 1024
VMEM_LIMIT_BYTES = V7X_VMEM_BYTES - 8 * 1024 * 1024
LANES = 128

ATTN_Q_ROWS = 4
ATTN_K_ROWS = ATTN_Q_ROWS + WIN_H

RING = 8
AHEAD = RING - 2
IDX_CHUNK = 4
PEER_STEP = 128
DOT_COPIES = 72

f32 = jnp.float32
bf16 = jnp.bfloat16


def _params(n_grid_dims):
    return pltpu.CompilerParams(
        dimension_semantics=("arbitrary",) * n_grid_dims,
        vmem_limit_bytes=VMEM_LIMIT_BYTES,
    )


def _rms(x):
    return x * lax.rsqrt(jnp.mean(x * x, axis=-1, keepdims=True) + EPS)


def _resident(shape, index_map):
    return pl.BlockSpec(shape, index_map, pipeline_mode=pl.Buffered(1))


def _ada_kernel(c_ref, w_ref, b_ref, o_ref):
    c = c_ref[...]
    cond = (c * jax.nn.sigmoid(c)).astype(bf16)
    o_ref[...] = jnp.dot(cond, w_ref[...].astype(bf16), preferred_element_type=f32) + b_ref[...]


def _ada(cvec, w, b):
    r, d = cvec.shape
    n = w.shape[1]
    tn = 1024
    assert n % tn == 0
    return pl.pallas_call(
        _ada_kernel,
        grid=(n // tn,),
        in_specs=[
            pl.BlockSpec((r, d), lambda j: (0, 0)),
            pl.BlockSpec((d, tn), lambda j: (0, j)),
            pl.BlockSpec((1, tn), lambda j: (0, j)),
        ],
        out_specs=pl.BlockSpec((r, tn), lambda j: (0, j)),
        out_shape=jax.ShapeDtypeStruct((r, n), f32),
        compiler_params=_params(1),
        name="ada_mod",
    )(cvec, w, b)


def _inproj_kernel(x_ref, sh_ref, sc_ref, g_ref, w_ref, hg_ref, o_ref, *, n_norm_heads, cw):
    x = x_ref[...]
    h = (_rms(x) * g_ref[...]) * (1.0 + sc_ref[...]) + sh_ref[...]
    hb = h.astype(bf16)
    n = o_ref.shape[1]
    for j in range(n // cw):
        acc = jnp.dot(hb, w_ref[:, j * cw:(j + 1) * cw], preferred_element_type=f32)
        for k in range(cw // HEAD_DIM):
            head = (j * cw) // HEAD_DIM + k
            blk = acc[:, k * HEAD_DIM:(k + 1) * HEAD_DIM]
            if head < n_norm_heads:
                blk = _rms(blk) * hg_ref[:, head * HEAD_DIM:(head + 1) * HEAD_DIM]
            c0 = head * HEAD_DIM
            o_ref[:, c0:c0 + HEAD_DIM] = blk.astype(bf16)


def _inproj(x2d, mod3, mod_row, g, w, hg, n_norm_heads, tm):
    m, d = x2d.shape
    n = w.shape[1]
    cw = 512
    assert m % tm == 0 and n % cw == 0
    kern = functools.partial(_inproj_kernel, n_norm_heads=n_norm_heads, cw=cw)
    return pl.pallas_call(
        kern,
        grid=(m // tm,),
        in_specs=[
            pl.BlockSpec((tm, d), lambda i: (i, 0)),
            pl.BlockSpec((None, 1, d), lambda i: (mod_row(i), 0, 0)),
            pl.BlockSpec((None, 1, d), lambda i: (mod_row(i), 0, 1)),
            pl.BlockSpec((1, d), lambda i: (0, 0)),
            _resident((d, n), lambda i: (0, 0)),
            pl.BlockSpec((1, hg.shape[1]), lambda i: (0, 0)),
        ],
        out_specs=pl.BlockSpec((tm, n), lambda i: (i, 0)),
        out_shape=jax.ShapeDtypeStruct((m, n), bf16),
        compiler_params=_params(1),
        name="in_proj",
    )(x2d, mod3, mod3, g, w, hg)


def _attn_bias(rpb, rows):
    kh = min(WIN_H, rows)
    n_heads = rpb.shape[0]
    col = np.arange(GRID_W)
    cs = np.clip(col - WIN_W // 2, 0, GRID_W - WIN_W)
    col_ok = (col[None, :] >= cs[:, None]) & (col[None, :] < cs[:, None] + WIN_W)
    dc = np.clip(col[None, :] - col[:, None], -(WIN_W - 1), WIN_W - 1) + (WIN_W - 1)
    onehot = dc[None] == np.arange(2 * WIN_W - 1)[:, None, None]
    by_dr = jnp.sum(jnp.where(onehot[None, None], rpb.astype(f32)[:, :, :, None, None], 0.0), axis=2)
    by_dr = jnp.where(col_ok[None, None], by_dr, NEG_INF)
    masked = jnp.full((n_heads, GRID_W, GRID_W), NEG_INF, f32)
    tiles = []
    for r0 in (0, ATTN_Q_ROWS, rows - ATTN_Q_ROWS):
        ws = int(np.clip(r0 - kh // 2, 0, rows - ATTN_K_ROWS))
        q_rows = []
        for r in range(r0, r0 + ATTN_Q_ROWS):
            rs = int(np.clip(r - kh // 2, 0, rows - kh))
            blocks = [by_dr[:, kr - r + (WIN_H - 1)] if rs <= kr < rs + kh else masked
                      for kr in range(ws, ws + ATTN_K_ROWS)]
            q_rows.append(jnp.concatenate(blocks, axis=2))
        tiles.append(jnp.concatenate(q_rows, axis=1))
    return jnp.stack(tiles, axis=1)


def _attn_kernel(q_ref, k_ref, v_ref, kc_ref, vc_ref, bias_ref, g_ref, o_ref, *, rows):
    grp = pl.program_id(2)
    ws = jnp.clip(ATTN_Q_ROWS * grp - WIN_H // 2, 0, rows - ATTN_K_ROWS)
    start = pl.multiple_of(ws * GRID_W, GRID_W)
    nk = ATTN_K_ROWS * GRID_W
    q = q_ref[...]
    kw = k_ref[pl.ds(start, nk), :]
    vw = v_ref[pl.ds(start, nk), :]
    nt = (((1,), (1,)), ((), ()))
    s_w = lax.dot_general(q, kw, nt, preferred_element_type=f32) + bias_ref[...]
    s_c = lax.dot_general(q, kc_ref[...], nt, preferred_element_type=f32)
    m = jnp.maximum(jnp.max(s_w, axis=-1, keepdims=True), jnp.max(s_c, axis=-1, keepdims=True))
    p_w = jnp.exp(s_w - m)
    p_c = jnp.exp(s_c - m)
    denom = jnp.sum(p_w, axis=-1, keepdims=True) + jnp.sum(p_c, axis=-1, keepdims=True)
    o = (jnp.dot(p_w.astype(bf16), vw, preferred_element_type=f32)
         + jnp.dot(p_c.astype(bf16), vc_ref[...], preferred_element_type=f32))
    o = o / denom
    o_ref[...] = (_rms(o) * g_ref[...]).astype(bf16)


def _attention(proj, kvc, bias, g_out, batch, seq, ctx_len, n_heads):
    rows = seq // GRID_W
    n_grp = rows // ATTN_Q_ROWS
    tq = ATTN_Q_ROWS * GRID_W
    nq_blk = seq // tq

    def cfg(g):
        return jnp.where(g == 0, 0, jnp.where(g == n_grp - 1, 2, 1))

    kern = functools.partial(_attn_kernel, rows=rows)
    return pl.pallas_call(
        kern,
        grid=(batch, n_heads, n_grp),
        in_specs=[
            pl.BlockSpec((tq, HEAD_DIM), lambda b, h, g: (b * nq_blk + g, h)),
            pl.BlockSpec((seq, HEAD_DIM), lambda b, h, g: (b, n_heads + h)),
            pl.BlockSpec((seq, HEAD_DIM), lambda b, h, g: (b, 2 * n_heads + h)),
            pl.BlockSpec((ctx_len, HEAD_DIM), lambda b, h, g: (b, h)),
            pl.BlockSpec((ctx_len, HEAD_DIM), lambda b, h, g: (b, n_heads + h)),
            pl.BlockSpec((None, None, tq, ATTN_K_ROWS * GRID_W), lambda b, h, g: (h, cfg(g), 0, 0)),
            pl.BlockSpec((None, 1, HEAD_DIM), lambda b, h, g: (h, 0, 0)),
        ],
        out_specs=pl.BlockSpec((tq, HEAD_DIM), lambda b, h, g: (b * nq_blk + g, h)),
        out_shape=jax.ShapeDtypeStruct((batch * seq, n_heads * HEAD_DIM), bf16),
        compiler_params=_params(3),
        name="nbr_attention",
    )(proj, proj, proj, kvc, kvc, bias, g_out)


HALO = 16


def _mix_kernel(oa_ref, bg_ref, cg_ref, u_ref, cgp_ref, up_ref, cgn_ref, un_ref, x_ref, wc_ref,
                gco_ref, wo_ref, gt1_ref, g2_ref, sh2_ref, sc2_ref, x1_ref, h2_ref, *, tm, seq):
    t0 = (pl.program_id(0) * tm) % seq
    z = cg_ref[...].astype(f32) * u_ref[...].astype(f32)
    zp = cgp_ref[HALO - 1:HALO, :].astype(f32) * up_ref[HALO - 1:HALO, :].astype(f32)
    zn = cgn_ref[0:1, :].astype(f32) * un_ref[0:1, :].astype(f32)
    zp = jnp.where(t0 > 0, zp, 0.0)
    zn = jnp.where(t0 + tm < seq, zn, 0.0)
    row = lax.broadcasted_iota(jnp.int32, (tm, 1), 0)
    z_m1 = jnp.where(row == 0, zp, pltpu.roll(z, 1, 0))
    z_p1 = jnp.where(row == tm - 1, zn, pltpu.roll(z, tm - 1, 0))
    y = wc_ref[0:1, :] * z_m1 + wc_ref[1:2, :] * z + wc_ref[2:3, :] * z_p1
    oc = bg_ref[...].astype(f32) * y
    pieces = []
    for gi in range(oc.shape[1] // LANES):
        blk = oc[:, gi * LANES:(gi + 1) * LANES]
        pieces.append((_rms(blk) * gco_ref[:, gi * LANES:(gi + 1) * LANES]).astype(bf16))
    ocn = jnp.concatenate(pieces, axis=1)
    da = oa_ref.shape[1]
    y2 = (jnp.dot(oa_ref[...], wo_ref[0:da, :], preferred_element_type=f32)
          + jnp.dot(ocn, wo_ref[da:, :], preferred_element_type=f32))
    x1 = x_ref[...] + gt1_ref[...] * y2
    x1_ref[...] = x1
    h2 = (_rms(x1) * g2_ref[...]) * (1.0 + sc2_ref[...]) + sh2_ref[...]
    h2_ref[...] = h2.astype(bf16)


def _mix(oattn, proj, x2d, mod3, w_conv, g_conv, w_out, g2, seq, tm):
    m, d = x2d.shape
    da = oattn.shape[1]
    dc = w_conv.shape[1]
    assert dc % LANES == 0 and da % dc == 0 and tm % HALO == 0 and seq % tm == 0
    cb = (3 * da) // dc
    hb = tm // HALO
    n_halo = m // HALO

    def brow(i):
        return (i * tm) // seq

    kern = functools.partial(_mix_kernel, tm=tm, seq=seq)
    return pl.pallas_call(
        kern,
        grid=(m // tm,),
        in_specs=[
            pl.BlockSpec((tm, da), lambda i: (i, 0)),
            pl.BlockSpec((tm, dc), lambda i: (i, cb)),
            pl.BlockSpec((tm, dc), lambda i: (i, cb + 1)),
            pl.BlockSpec((tm, dc), lambda i: (i, cb + 2)),
            pl.BlockSpec((HALO, dc), lambda i: (jnp.maximum(i * hb - 1, 0), cb + 1)),
            pl.BlockSpec((HALO, dc), lambda i: (jnp.maximum(i * hb - 1, 0), cb + 2)),
            pl.BlockSpec((HALO, dc), lambda i: (jnp.minimum((i + 1) * hb, n_halo - 1), cb + 1)),
            pl.BlockSpec((HALO, dc), lambda i: (jnp.minimum((i + 1) * hb, n_halo - 1), cb + 2)),
            pl.BlockSpec((tm, d), lambda i: (i, 0)),
            pl.BlockSpec((CONV_W, dc), lambda i: (0, 0)),
            pl.BlockSpec((1, dc), lambda i: (0, 0)),
            _resident((da + dc, d), lambda i: (0, 0)),
            pl.BlockSpec((None, 1, d), lambda i: (brow(i), 0, 2)),
            pl.BlockSpec((1, d), lambda i: (0, 0)),
            pl.BlockSpec((None, 1, d), lambda i: (brow(i), 0, 3)),
            pl.BlockSpec((None, 1, d), lambda i: (brow(i), 0, 4)),
        ],
        out_specs=[
            pl.BlockSpec((tm, d), lambda i: (i, 0)),
            pl.BlockSpec((tm, d), lambda i: (i, 0)),
        ],
        out_shape=[jax.ShapeDtypeStruct((m, d), f32), jax.ShapeDtypeStruct((m, d), bf16)],
        compiler_params=_params(1),
        name="conv_out_proj",
    )(oattn, proj, proj, proj, proj, proj, proj, proj, x2d, w_conv, g_conv, w_out,
      mod3, g2, mod3, mod3)


def _cand_layout():
    pos = [k2 for k2 in range(PEER_TOPK)]
    for k1 in range(1, 8):
        pos += [k1 * PEER_TOPK + k2 for k2 in range(8)]
    pos += [k1 * PEER_TOPK for k1 in range(8, PEER_TOPK)]
    return np.asarray(pos, np.float32)


N_CAND = 16 + 7 * 8 + 8


def _top16_desc(s, iota_k, iota_t):
    n_keys = s.shape[0]

    def body(t, carry):
        s, vals, idxs = carry
        m = jnp.max(s, axis=0, keepdims=True)
        idx = jnp.min(jnp.where(s == m, iota_k, float(n_keys)), axis=0, keepdims=True)
        s = jnp.where(iota_k == idx, -jnp.inf, s)
        hit = iota_t == t
        return s, jnp.where(hit, m, vals), jnp.where(hit, idx, idxs)

    zeros = jnp.zeros((PEER_TOPK, s.shape[1]), f32)
    _, vals, idxs = lax.fori_loop(0, PEER_TOPK, body, (s, zeros, zeros))
    return vals, idxs


def _route_kernel(h2_ref, wq_ref, sk_ref, cpos_ref, idx_ref, gate_ref, q_scr, e_scr, g_scr,
                  *, n_heads, tm):
    n_keys = sk_ref.shape[1]
    qf = jnp.dot(h2_ref[...], wq_ref[...], preferred_element_type=f32).astype(bf16)
    for hp in range(2 * n_heads):
        q_scr[hp] = qf[:, hp * HEAD_DIM:(hp + 1) * HEAD_DIM]

    iota_k = lax.broadcasted_iota(jnp.int32, (n_keys, 2 * LANES), 0).astype(f32)
    iota_t2 = lax.broadcasted_iota(jnp.int32, (PEER_TOPK, 2 * LANES), 0)
    iota_t = lax.broadcasted_iota(jnp.int32, (PEER_TOPK, LANES), 0)
    cpos = cpos_ref[...]
    nt = (((1,), (1,)), ((), ()))

    for lb in range(tm // LANES):
        def head_body(h, _):
            qa = q_scr[2 * h, lb * LANES:(lb + 1) * LANES, :]
            qb = q_scr[2 * h + 1, lb * LANES:(lb + 1) * LANES, :]
            s1 = lax.dot_general(sk_ref[2 * h], qa, nt, preferred_element_type=f32)
            s2 = lax.dot_general(sk_ref[2 * h + 1], qb, nt, preferred_element_type=f32)
            a12, i12 = _top16_desc(jnp.concatenate([s1, s2], axis=1), iota_k, iota_t2)
            a1, a2 = a12[:, :LANES], a12[:, LANES:]
            i1, i2 = i12[:, :LANES], i12[:, LANES:]
            pieces = [a1[0:1, :] + a2]
            for k1 in range(1, 8):
                pieces.append(a1[k1:k1 + 1, :] + a2[0:8, :])
            pieces.append(a1[8:16, :] + a2[0:1, :])
            cand = jnp.concatenate(pieces, axis=0)

            def pick(t, carry):
                cand, best, bpos = carry
                m = jnp.max(cand, axis=0, keepdims=True)
                p = jnp.min(jnp.where(cand == m, cpos, 1e9), axis=0, keepdims=True)
                cand = jnp.where(cpos == p, -jnp.inf, cand)
                hit = iota_t == t
                return cand, jnp.where(hit, m, best), jnp.where(hit, p, bpos)

            zeros = jnp.zeros((PEER_TOPK, LANES), f32)
            _, best, bpos = lax.fori_loop(0, PEER_TOPK, pick, (cand, zeros, zeros))
            k1f = jnp.floor(bpos * (1.0 / PEER_TOPK))
            k2f = bpos - PEER_TOPK * k1f
            e1 = jnp.zeros_like(bpos)
            e2 = jnp.zeros_like(bpos)
            for j in range(PEER_TOPK):
                e1 = e1 + jnp.where(k1f == float(j), i1[j:j + 1, :], 0.0)
                e2 = e2 + jnp.where(k2f == float(j), i2[j:j + 1, :], 0.0)
            expert = e1 * float(n_keys) + e2
            ex = jnp.exp(best - best[0:1, :])
            gate = ex / jnp.sum(ex, axis=0, keepdims=True)
            r0 = pl.multiple_of(h * PEER_TOPK, PEER_TOPK)
            e_scr[pl.ds(r0, PEER_TOPK), :] = expert
            g_scr[pl.ds(r0, PEER_TOPK), :] = gate
            return 0

        lax.fori_loop(0, n_heads, head_body, 0)
        idx_ref[lb * LANES:(lb + 1) * LANES, :] = e_scr[...].T.astype(jnp.int32)
        gate_ref[:, lb * LANES:(lb + 1) * LANES] = g_scr[...]


def _route(h2, w_pq, sub_keys2, n_heads, tm):
    m, d = h2.shape
    nq = w_pq.shape[1]
    n_keys = sub_keys2.shape[1]
    ne = n_heads * PEER_TOPK
    assert ne == LANES and tm % LANES == 0 and m % tm == 0
    cpos = jnp.asarray(np.broadcast_to(_cand_layout()[:, None], (N_CAND, LANES)).copy())
    kern = functools.partial(_route_kernel, n_heads=n_heads, tm=tm)
    return pl.pallas_call(
        kern,
        grid=(m // tm,),
        in_specs=[
            pl.BlockSpec((tm, d), lambda i: (i, 0)),
            _resident((d, nq), lambda i: (0, 0)),
            _resident((2 * n_heads, n_keys, HEAD_DIM), lambda i: (0, 0, 0)),
            pl.BlockSpec((N_CAND, LANES), lambda i: (0, 0)),
        ],
        out_specs=[
            pl.BlockSpec((tm, ne), lambda i: (i, 0)),
            pl.BlockSpec((ne, tm), lambda i: (0, i)),
        ],
        out_shape=[jax.ShapeDtypeStruct((m, ne), jnp.int32), jax.ShapeDtypeStruct((ne, m), f32)],
        scratch_shapes=[
            pltpu.VMEM((2 * n_heads, tm, HEAD_DIM), bf16),
            pltpu.VMEM((ne, LANES), f32),
            pltpu.VMEM((ne, LANES), f32),
        ],
        compiler_params=_params(1),
        name="peer_route",
    )(h2, w_pq, sub_keys2, cpos)


SLAB = 16


def _sublane_sums(r, sub):
    def merge(a, b, half):
        lo = (sub % (2 * half)) < half
        x = jnp.where(lo, a, pltpu.roll(b, half, 0))
        y = jnp.where(lo, pltpu.roll(a, 8 - half, 0), b)
        return x + y
    a = [merge(r[j], r[j + 4], 4) for j in range(4)]
    b = [merge(a[j], a[j + 2], 2) for j in range(2)]
    return merge(b[0], b[1], 1)


def _expert_kernel(idx_hbm, uv_hbm, gate_ref, h2_ref, x1_ref, gt2_ref, o_ref, idx_smem, *rest, d, ne):
    rings, acts = rest[:RING], rest[RING:2 * RING]
    hs_scr, ys_scr, idx_sem, row_sem = rest[2 * RING:]
    step = pl.program_id(0)
    n_steps = pl.num_programs(0)
    n_tiles = d // LANES
    n_groups = ne // 8
    last_chunk = n_steps * (PEER_STEP // IDX_CHUNK) - 1
    lane = lax.broadcasted_iota(jnp.int32, (ne, PEER_STEP), 1)
    sub = lax.broadcasted_iota(jnp.int32, (8, LANES), 0)

    def idx_copy(chunk, buf):
        c = jnp.minimum(chunk, last_chunk)
        return pltpu.make_async_copy(idx_hbm.at[pl.ds(c * IDX_CHUNK, IDX_CHUNK)], idx_smem.at[buf],
                                     idx_sem.at[buf])

    def row_copy(t, e):
        buf, r = (t // IDX_CHUNK) % 2, t % IDX_CHUNK
        return pltpu.make_async_copy(uv_hbm.at[idx_smem[buf, r, e]], rings[t % RING].at[e],
                                     row_sem.at[t % RING])

    def wait_rows(slot):
        pltpu.make_async_copy(uv_hbm.at[pl.ds(0, ne)], rings[slot], row_sem.at[slot]).wait()

    @pl.when(step == 0)
    def _():
        idx_copy(0, 0).start()
        idx_copy(1, 1).start()
        idx_copy(0, 0).wait()
        idx_copy(1, 1).wait()

        def prime(e, _):
            for t in range(AHEAD):
                row_copy(t, e).start()
            return 0
        lax.fori_loop(0, ne, prime, 0)
        idx_copy(2, 0).start()

    hf = h2_ref[...].astype(f32)
    for rg in range(PEER_STEP // 8):
        for j in range(n_tiles):
            hs_scr[pl.ds(rg * 8 * n_tiles + j, 8, stride=n_tiles), :] = (
                hf[rg * 8:(rg + 1) * 8, j * LANES:(j + 1) * LANES])

    def ring_body(it, _):
        t0 = it * RING
        first_chunk = (step * PEER_STEP + t0) // IDX_CHUNK

        def start_copies(k, lo, hi):
            for e in range(lo, hi):
                row_copy(k + AHEAD, e).start(priority=e % 2)

        def open_token(k):
            if (k + AHEAD) % IDX_CHUNK == 0:
                q = (k + AHEAD) // IDX_CHUNK
                idx_copy(first_chunk + q, q % 2).wait()
                idx_copy(first_chunk + q + 1, 1 - q % 2).start()
            wait_rows(k)
            gcol = jnp.sum(jnp.where(lane == t0 + k, gate_ref[...], 0.0), axis=1, keepdims=True)
            hs = hs_scr[pl.ds(pl.multiple_of((t0 + k) * n_tiles, n_tiles), n_tiles), :]
            return gcol, hs

        def dot(k, hs, groups, n_copies):
            cols = []
            for g in groups:
                parts = []
                for j in range(8):
                    p = rings[k][g * 8 + j, 0:SLAB, :].astype(f32) * hs
                    parts.append(p[0:8, :] + p[8:16, :])
                cols.append(jnp.sum(_sublane_sums(parts, sub), axis=1, keepdims=True))
                start_copies(k, g * n_copies // n_groups, (g + 1) * n_copies // n_groups)
            return cols

        def activate(k, cols, gcol):
            s = jnp.concatenate(cols, axis=0)
            act = 0.5 * s * (1.0 + lax.erf(s * (2.0 ** -0.5))) * gcol
            acts[k][...] = jnp.broadcast_to(act, (ne, LANES))

        def axpy(k, stage, first_copy):
            accs = [jnp.zeros((SLAB, LANES), f32) for _ in range(4)]
            n_copies = ne - first_copy
            for g in range(n_groups):
                for e in range(g * 8, g * 8 + 8):
                    v = rings[k][e, SLAB:2 * SLAB, :].astype(f32)
                    accs[e % 4] = accs[e % 4] + acts[k][e:e + 1, :] * v
                start_copies(stage, first_copy + g * n_copies // n_groups,
                             first_copy + (g + 1) * n_copies // n_groups)
            ys_scr[pl.ds(pl.multiple_of((t0 + k) * n_tiles, n_tiles), n_tiles), :] = (
                (accs[0] + accs[1]) + (accs[2] + accs[3]))

        half = n_groups // 2
        gcol, hs = open_token(0)
        cols = dot(0, hs, range(n_groups), ne)
        for k in range(1, RING):
            gcol_k, hs = open_token(k)
            cols_k = dot(k, hs, range(half), DOT_COPIES)
            activate(k - 1, cols, gcol)
            cols_k += dot(k, hs, range(half, n_groups), DOT_COPIES)
            axpy(k - 1, k, DOT_COPIES)
            cols, gcol = cols_k, gcol_k
        activate(RING - 1, cols, gcol)
        axpy(RING - 1, RING - 1, ne)
        return 0

    lax.fori_loop(0, PEER_STEP // RING, ring_body, 0)

    for rg in range(PEER_STEP // 8):
        y = jnp.concatenate([ys_scr[pl.ds(rg * 8 * n_tiles + j, 8, stride=n_tiles), :]
                             for j in range(n_tiles)], axis=1)
        o_ref[rg * 8:(rg + 1) * 8, :] = x1_ref[rg * 8:(rg + 1) * 8, :] + gt2_ref[...] * y

    @pl.when(step == n_steps - 1)
    def _():
        for t in range(AHEAD):
            wait_rows(t % RING)
        last_fetch_stage = max(k for k in range(RING) if (k + AHEAD) % IDX_CHUNK == 0)
        idx_copy(last_chunk, 1 - ((last_fetch_stage + AHEAD) // IDX_CHUNK) % 2).wait()


def _experts(idx, gate_t, h2, x1, mod3, uv, seq):
    m, d = h2.shape
    ne = idx.shape[1]
    assert d == SLAB * LANES and uv.shape[1:] == (2 * SLAB, LANES) and ne % 8 == 0
    assert m % PEER_STEP == 0 and seq % PEER_STEP == 0
    assert RING % (2 * IDX_CHUNK) == 0 and PEER_STEP % RING == 0 and AHEAD <= 2 * IDX_CHUNK
    kern = functools.partial(_expert_kernel, d=d, ne=ne)
    return pl.pallas_call(
        kern,
        grid=(m // PEER_STEP,),
        in_specs=[
            pl.BlockSpec(memory_space=pl.ANY),
            pl.BlockSpec(memory_space=pl.ANY),
            pl.BlockSpec((ne, PEER_STEP), lambda i: (0, i)),
            pl.BlockSpec((PEER_STEP, d), lambda i: (i, 0)),
            pl.BlockSpec((PEER_STEP, d), lambda i: (i, 0)),
            pl.BlockSpec((None, 1, d), lambda i: ((i * PEER_STEP) // seq, 0, 5)),
        ],
        out_specs=pl.BlockSpec((PEER_STEP, d), lambda i: (i, 0)),
        out_shape=jax.ShapeDtypeStruct((m, d), f32),
        scratch_shapes=(
            [pltpu.SMEM((2, IDX_CHUNK, ne), jnp.int32)]
            + [pltpu.VMEM((ne, 2 * SLAB, LANES), bf16) for _ in range(RING)]
            + [pltpu.VMEM((ne, LANES), f32) for _ in range(RING)]
            + [pltpu.VMEM((PEER_STEP * SLAB, LANES), f32),
               pltpu.VMEM((PEER_STEP * SLAB, LANES), f32),
               pltpu.SemaphoreType.DMA((2,)),
               pltpu.SemaphoreType.DMA((RING,))]),
        compiler_params=_params(1),
        name="peer_experts",
    )(idx, uv, gate_t, h2, x1, mod3)


def kernel(x, c, ctx, c_ctx, w_ada, b_ada, g_norm1, w_in, q_norm, k_norm, rpb, w_conv,
           g_attn_out, g_conv_out, w_out, g_norm2, w_pq, sub_keys, u_experts, v_experts):
    batch, seq, d = x.shape
    ctx_len = ctx.shape[1]
    depth = w_ada.shape[0]
    n_heads = rpb.shape[1]
    d_attn = n_heads * HEAD_DIM
    d_conv = w_conv.shape[2]
    p_heads = sub_keys.shape[1]
    assert d % LANES == 0 and seq % (ATTN_Q_ROWS * GRID_W) == 0 and seq // GRID_W >= ATTN_K_ROWS
    assert w_in.shape[2] == 3 * d_attn + 3 * d_conv and g_conv_out.shape[2] == LANES
    assert sub_keys.shape[2] == 2 and sub_keys.shape[4] == HEAD_DIM
    tm = 256

    x2d = x.reshape(batch * seq, d)
    ctx2d = ctx.reshape(batch * ctx_len, d)
    mod_rows = -(-(batch + 1) // 8) * 8
    cvec = jnp.zeros((mod_rows, d), f32).at[:batch].set(c).at[batch].set(c_ctx)

    for l in range(depth):
        assert l == depth - 1
        mod = _ada(cvec, w_ada[l], b_ada[l][None, :])
        mod3 = mod.reshape(mod_rows, 1, mod.shape[1])

        w_in_b = w_in[l].astype(bf16)
        g1 = g_norm1[l][None, :]
        hg = jnp.concatenate([jnp.tile(q_norm[l], n_heads) * (HEAD_DIM ** -0.5),
                              jnp.tile(k_norm[l], n_heads)])[None, :]
        proj = _inproj(x2d, mod3, lambda i: (i * tm) // seq, g1, w_in_b, hg, 2 * n_heads, tm)
        hgc = jnp.tile(k_norm[l], n_heads)[None, :]
        kvc = _inproj(ctx2d, mod3, lambda i: batch, g1, w_in_b[:, d_attn:3 * d_attn], hgc,
                      n_heads, tm)

        bias = _attn_bias(rpb[l], seq // GRID_W)
        oattn = _attention(proj, kvc, bias, g_attn_out[l][:, None, :], batch, seq, ctx_len, n_heads)

        x1, h2 = _mix(oattn, proj, x2d, mod3, w_conv[l], g_conv_out[l].reshape(1, d_conv),
                      w_out[l].astype(bf16), g_norm2[l][None, :], seq, tm)

        sk2 = sub_keys[l].reshape(2 * p_heads, sub_keys.shape[3], HEAD_DIM).astype(bf16)
        idx, gate_t = _route(h2, w_pq[l].astype(bf16), sk2, p_heads, tm)

        n_exp = u_experts.shape[1]
        uv = jnp.concatenate([u_experts[l].astype(bf16).reshape(n_exp, SLAB, LANES),
                              v_experts[l].astype(bf16).reshape(n_exp, SLAB, LANES)], axis=1)
        x2d = _experts(idx, gate_t, h2, x1, mod3, uv, seq)
    return x2d.reshape(batch, seq, d)
```

```python
import functools

import numpy as np
import jax
import jax.numpy as jnp
from jax import lax
from jax.experimental import pallas as pl
from jax.experimental.pallas import tpu as pltpu

HEAD_DIM = 128
GRID_W = 64
WIN_H = 8
WIN_W = 16
CONV_W = 3
PEER_TOPK = 16
EPS = 1e-6
NEG_INF = -1e30

V7X_VMEM_BYTES = 64 * 1024 * 1024
VMEM_LIMIT_BYTES = V7X_VMEM_BYTES - 8 * 1024 * 1024
LANES = 128

ATTN_Q_ROWS = 4
ATTN_K_ROWS = ATTN_Q_ROWS + WIN_H

RING = 8
AHEAD = RING - 2
IDX_CHUNK = 4
PEER_STEP = 128
DOT_COPIES = 72

f32 = jnp.float32
bf16 = jnp.bfloat16


def _params(n_grid_dims):
    return pltpu.CompilerParams(
        dimension_semantics=("arbitrary",) * n_grid_dims,
        vmem_limit_bytes=VMEM_LIMIT_BYTES,
    )


def _rms(x):
    return x * lax.rsqrt(jnp.mean(x * x, axis=-1, keepdims=True) + EPS)


def _resident(shape, index_map):
    return pl.BlockSpec(shape, index_map, pipeline_mode=pl.Buffered(1))


def _ada_kernel(c_ref, w_ref, b_ref, o_ref):
    c = c_ref[...]
    cond = (c * jax.nn.sigmoid(c)).astype(bf16)
    o_ref[...] = jnp.dot(cond, w_ref[...].astype(bf16), preferred_element_type=f32) + b_ref[...]


def _ada(cvec, w, b):
    r, d = cvec.shape
    n = w.shape[1]
    tn = 1024
    assert n % tn == 0
    return pl.pallas_call(
        _ada_kernel,
        grid=(n // tn,),
        in_specs=[
            pl.BlockSpec((r, d), lambda j: (0, 0)),
            pl.BlockSpec((d, tn), lambda j: (0, j)),
            pl.BlockSpec((1, tn), lambda j: (0, j)),
        ],
        out_specs=pl.BlockSpec((r, tn), lambda j: (0, j)),
        out_shape=jax.ShapeDtypeStruct((r, n), f32),
        compiler_params=_params(1),
        name="ada_mod",
    )(cvec, w, b)


def _inproj_kernel(x_ref, sh_ref, sc_ref, g_ref, w_ref, hg_ref, o_ref, *, n_norm_heads, cw):
    x = x_ref[...]
    h = (_rms(x) * g_ref[...]) * (1.0 + sc_ref[...]) + sh_ref[...]
    hb = h.astype(bf16)
    n = o_ref.shape[1]
    for j in range(n // cw):
        acc = jnp.dot(hb, w_ref[:, j * cw:(j + 1) * cw], preferred_element_type=f32)
        for k in range(cw // HEAD_DIM):
            head = (j * cw) // HEAD_DIM + k
            blk = acc[:, k * HEAD_DIM:(k + 1) * HEAD_DIM]
            if head < n_norm_heads:
                blk = _rms(blk) * hg_ref[:, head * HEAD_DIM:(head + 1) * HEAD_DIM]
            c0 = head * HEAD_DIM
            o_ref[:, c0:c0 + HEAD_DIM] = blk.astype(bf16)


def _inproj(x2d, mod3, mod_row, g, w, hg, n_norm_heads, tm):
    m, d = x2d.shape
    n = w.shape[1]
    cw = 512
    assert m % tm == 0 and n % cw == 0
    kern = functools.partial(_inproj_kernel, n_norm_heads=n_norm_heads, cw=cw)
    return pl.pallas_call(
        kern,
        grid=(m // tm,),
        in_specs=[
            pl.BlockSpec((tm, d), lambda i: (i, 0)),
            pl.BlockSpec((None, 1, d), lambda i: (mod_row(i), 0, 0)),
            pl.BlockSpec((None, 1, d), lambda i: (mod_row(i), 0, 1)),
            pl.BlockSpec((1, d), lambda i: (0, 0)),
            _resident((d, n), lambda i: (0, 0)),
            pl.BlockSpec((1, hg.shape[1]), lambda i: (0, 0)),
        ],
        out_specs=pl.BlockSpec((tm, n), lambda i: (i, 0)),
        out_shape=jax.ShapeDtypeStruct((m, n), bf16),
        compiler_params=_params(1),
        name="in_proj",
    )(x2d, mod3, mod3, g, w, hg)


def _attn_bias(rpb, rows):
    kh = min(WIN_H, rows)
    n_heads = rpb.shape[0]
    col = np.arange(GRID_W)
    cs = np.clip(col - WIN_W // 2, 0, GRID_W - WIN_W)
    col_ok = (col[None, :] >= cs[:, None]) & (col[None, :] < cs[:, None] + WIN_W)
    dc = np.clip(col[None, :] - col[:, None], -(WIN_W - 1), WIN_W - 1) + (WIN_W - 1)
    onehot = dc[None] == np.arange(2 * WIN_W - 1)[:, None, None]
    by_dr = jnp.sum(jnp.where(onehot[None, None], rpb.astype(f32)[:, :, :, None, None], 0.0), axis=2)
    by_dr = jnp.where(col_ok[None, None], by_dr, NEG_INF)
    masked = jnp.full((n_heads, GRID_W, GRID_W), NEG_INF, f32)
    tiles = []
    for r0 in (0, ATTN_Q_ROWS, rows - ATTN_Q_ROWS):
        ws = int(np.clip(r0 - kh // 2, 0, rows - ATTN_K_ROWS))
        q_rows = []
        for r in range(r0, r0 + ATTN_Q_ROWS):
            rs = int(np.clip(r - kh // 2, 0, rows - kh))
            blocks = [by_dr[:, kr - r + (WIN_H - 1)] if rs <= kr < rs + kh else masked
                      for kr in range(ws, ws + ATTN_K_ROWS)]
            q_rows.append(jnp.concatenate(blocks, axis=2))
        tiles.append(jnp.concatenate(q_rows, axis=1))
    return jnp.stack(tiles, axis=1)


def _attn_kernel(q_ref, k_ref, v_ref, kc_ref, vc_ref, bias_ref, g_ref, o_ref, *, rows):
    grp = pl.program_id(2)
    ws = jnp.clip(ATTN_Q_ROWS * grp - WIN_H // 2, 0, rows - ATTN_K_ROWS)
    start = pl.multiple_of(ws * GRID_W, GRID_W)
    nk = ATTN_K_ROWS * GRID_W
    q = q_ref[...]
    kw = k_ref[pl.ds(start, nk), :]
    vw = v_ref[pl.ds(start, nk), :]
    nt = (((1,), (1,)), ((), ()))
    s_w = lax.dot_general(q, kw, nt, preferred_element_type=f32) + bias_ref[...]
    s_c = lax.dot_general(q, kc_ref[...], nt, preferred_element_type=f32)
    m = jnp.maximum(jnp.max(s_w, axis=-1, keepdims=True), jnp.max(s_c, axis=-1, keepdims=True))
    p_w = jnp.exp(s_w - m)
    p_c = jnp.exp(s_c - m)
    denom = jnp.sum(p_w, axis=-1, keepdims=True) + jnp.sum(p_c, axis=-1, keepdims=True)
    o = (jnp.dot(p_w.astype(bf16), vw, preferred_element_type=f32)
         + jnp.dot(p_c.astype(bf16), vc_ref[...], preferred_element_type=f32))
    o = o / denom
    o_ref[...] = (_rms(o) * g_ref[...]).astype(bf16)


def _attention(proj, kvc, bias, g_out, batch, seq, ctx_len, n_heads):
    rows = seq // GRID_W
    n_grp = rows // ATTN_Q_ROWS
    tq = ATTN_Q_ROWS * GRID_W
    nq_blk = seq // tq

    def cfg(g):
        return jnp.where(g == 0, 0, jnp.where(g == n_grp - 1, 2, 1))

    kern = functools.partial(_attn_kernel, rows=rows)
    return pl.pallas_call(
        kern,
        grid=(batch, n_heads, n_grp),
        in_specs=[
            pl.BlockSpec((tq, HEAD_DIM), lambda b, h, g: (b * nq_blk + g, h)),
            pl.BlockSpec((seq, HEAD_DIM), lambda b, h, g: (b, n_heads + h)),
            pl.BlockSpec((seq, HEAD_DIM), lambda b, h, g: (b, 2 * n_heads + h)),
            pl.BlockSpec((ctx_len, HEAD_DIM), lambda b, h, g: (b, h)),
            pl.BlockSpec((ctx_len, HEAD_DIM), lambda b, h, g: (b, n_heads + h)),
            pl.BlockSpec((None, None, tq, ATTN_K_ROWS * GRID_W), lambda b, h, g: (h, cfg(g), 0, 0)),
            pl.BlockSpec((None, 1, HEAD_DIM), lambda b, h, g: (h, 0, 0)),
        ],
        out_specs=pl.BlockSpec((tq, HEAD_DIM), lambda b, h, g: (b * nq_blk + g, h)),
        out_shape=jax.ShapeDtypeStruct((batch * seq, n_heads * HEAD_DIM), bf16),
        compiler_params=_params(3),
        name="nbr_attention",
    )(proj, proj, proj, kvc, kvc, bias, g_out)


HALO = 16


def _mix_kernel(oa_ref, bg_ref, cg_ref, u_ref, cgp_ref, up_ref, cgn_ref, un_ref, x_ref, wc_ref,
                gco_ref, wo_ref, gt1_ref, g2_ref, sh2_ref, sc2_ref, x1_ref, h2_ref, *, tm, seq):
    t0 = (pl.program_id(0) * tm) % seq
    z = cg_ref[...].astype(f32) * u_ref[...].astype(f32)
    zp = cgp_ref[HALO - 1:HALO, :].astype(f32) * up_ref[HALO - 1:HALO, :].astype(f32)
    zn = cgn_ref[0:1, :].astype(f32) * un_ref[0:1, :].astype(f32)
    zp = jnp.where(t0 > 0, zp, 0.0)
    zn = jnp.where(t0 + tm < seq, zn, 0.0)
    row = lax.broadcasted_iota(jnp.int32, (tm, 1), 0)
    z_m1 = jnp.where(row == 0, zp, pltpu.roll(z, 1, 0))
    z_p1 = jnp.where(row == tm - 1, zn, pltpu.roll(z, tm - 1, 0))
    y = wc_ref[0:1, :] * z_m1 + wc_ref[1:2, :] * z + wc_ref[2:3, :] * z_p1
    oc = bg_ref[...].astype(f32) * y
    pieces = []
    for gi in range(oc.shape[1] // LANES):
        blk = oc[:, gi * LANES:(gi + 1) * LANES]
        pieces.append((_rms(blk) * gco_ref[:, gi * LANES:(gi + 1) * LANES]).astype(bf16))
    ocn = jnp.concatenate(pieces, axis=1)
    da = oa_ref.shape[1]
    y2 = (jnp.dot(oa_ref[...], wo_ref[0:da, :], preferred_element_type=f32)
          + jnp.dot(ocn, wo_ref[da:, :], preferred_element_type=f32))
    x1 = x_ref[...] + gt1_ref[...] * y2
    x1_ref[...] = x1
    h2 = (_rms(x1) * g2_ref[...]) * (1.0 + sc2_ref[...]) + sh2_ref[...]
    h2_ref[...] = h2.astype(bf16)


def _mix(oattn, proj, x2d, mod3, w_conv, g_conv, w_out, g2, seq, tm):
    m, d = x2d.shape
    da = oattn.shape[1]
    dc = w_conv.shape[1]
    assert dc % LANES == 0 and da % dc == 0 and tm % HALO == 0 and seq % tm == 0
    cb = (3 * da) // dc
    hb = tm // HALO
    n_halo = m // HALO

    def brow(i):
        return (i * tm) // seq

    kern = functools.partial(_mix_kernel, tm=tm, seq=seq)
    return pl.pallas_call(
        kern,
        grid=(m // tm,),
        in_specs=[
            pl.BlockSpec((tm, da), lambda i: (i, 0)),
            pl.BlockSpec((tm, dc), lambda i: (i, cb)),
            pl.BlockSpec((tm, dc), lambda i: (i, cb + 1)),
            pl.BlockSpec((tm, dc), lambda i: (i, cb + 2)),
            pl.BlockSpec((HALO, dc), lambda i: (jnp.maximum(i * hb - 1, 0), cb + 1)),
            pl.BlockSpec((HALO, dc), lambda i: (jnp.maximum(i * hb - 1, 0), cb + 2)),
            pl.BlockSpec((HALO, dc), lambda i: (jnp.minimum((i + 1) * hb, n_halo - 1), cb + 1)),
            pl.BlockSpec((HALO, dc), lambda i: (jnp.minimum((i + 1) * hb, n_halo - 1), cb + 2)),
            pl.BlockSpec((tm, d), lambda i: (i, 0)),
            pl.BlockSpec((CONV_W, dc), lambda i: (0, 0)),
            pl.BlockSpec((1, dc), lambda i: (0, 0)),
            _resident((da + dc, d), lambda i: (0, 0)),
            pl.BlockSpec((None, 1, d), lambda i: (brow(i), 0, 2)),
            pl.BlockSpec((1, d), lambda i: (0, 0)),
            pl.BlockSpec((None, 1, d), lambda i: (brow(i), 0, 3)),
            pl.BlockSpec((None, 1, d), lambda i: (brow(i), 0, 4)),
        ],
        out_specs=[
            pl.BlockSpec((tm, d), lambda i: (i, 0)),
            pl.BlockSpec((tm, d), lambda i: (i, 0)),
        ],
        out_shape=[jax.ShapeDtypeStruct((m, d), f32), jax.ShapeDtypeStruct((m, d), bf16)],
        compiler_params=_params(1),
        name="conv_out_proj",
    )(oattn, proj, proj, proj, proj, proj, proj, proj, x2d, w_conv, g_conv, w_out,
      mod3, g2, mod3, mod3)


def _cand_layout():
    pos = [k2 for k2 in range(PEER_TOPK)]
    for k1 in range(1, 8):
        pos += [k1 * PEER_TOPK + k2 for k2 in range(8)]
    pos += [k1 * PEER_TOPK for k1 in range(8, PEER_TOPK)]
    return np.asarray(pos, np.float32)


N_CAND = 16 + 7 * 8 + 8


def _top16_desc(s, iota_k, iota_t):
    n_keys = s.shape[0]

    def body(t, carry):
        s, vals, idxs = carry
        m = jnp.max(s, axis=0, keepdims=True)
        idx = jnp.min(jnp.where(s == m, iota_k, float(n_keys)), axis=0, keepdims=True)
        s = jnp.where(iota_k == idx, -jnp.inf, s)
        hit = iota_t == t
        return s, jnp.where(hit, m, vals), jnp.where(hit, idx, idxs)

    zeros = jnp.zeros((PEER_TOPK, s.shape[1]), f32)
    _, vals, idxs = lax.fori_loop(0, PEER_TOPK, body, (s, zeros, zeros))
    return vals, idxs


def _route_kernel(h2_ref, wq_ref, sk_ref, cpos_ref, idx_ref, gate_ref, q_scr, e_scr, g_scr,
                  *, n_heads, tm):
    n_keys = sk_ref.shape[1]
    qf = jnp.dot(h2_ref[...], wq_ref[...], preferred_element_type=f32).astype(bf16)
    for hp in range(2 * n_heads):
        q_scr[hp] = qf[:, hp * HEAD_DIM:(hp + 1) * HEAD_DIM]

    iota_k = lax.broadcasted_iota(jnp.int32, (n_keys, 2 * LANES), 0).astype(f32)
    iota_t2 = lax.broadcasted_iota(jnp.int32, (PEER_TOPK, 2 * LANES), 0)
    iota_t = lax.broadcasted_iota(jnp.int32, (PEER_TOPK, LANES), 0)
    cpos = cpos_ref[...]
    nt = (((1,), (1,)), ((), ()))

    for lb in range(tm // LANES):
        def head_body(h, _):
            qa = q_scr[2 * h, lb * LANES:(lb + 1) * LANES, :]
            qb = q_scr[2 * h + 1, lb * LANES:(lb + 1) * LANES, :]
            s1 = lax.dot_general(sk_ref[2 * h], qa, nt, preferred_element_type=f32)
            s2 = lax.dot_general(sk_ref[2 * h + 1], qb, nt, preferred_element_type=f32)
            a12, i12 = _top16_desc(jnp.concatenate([s1, s2], axis=1), iota_k, iota_t2)
            a1, a2 = a12[:, :LANES], a12[:, LANES:]
            i1, i2 = i12[:, :LANES], i12[:, LANES:]
            pieces = [a1[0:1, :] + a2]
            for k1 in range(1, 8):
                pieces.append(a1[k1:k1 + 1, :] + a2[0:8, :])
            pieces.append(a1[8:16, :] + a2[0:1, :])
            cand = jnp.concatenate(pieces, axis=0)

            def pick(t, carry):
                cand, best, bpos = carry
                m = jnp.max(cand, axis=0, keepdims=True)
                p = jnp.min(jnp.where(cand == m, cpos, 1e9), axis=0, keepdims=True)
                cand = jnp.where(cpos == p, -jnp.inf, cand)
                hit = iota_t == t
                return cand, jnp.where(hit, m, best), jnp.where(hit, p, bpos)

            zeros = jnp.zeros((PEER_TOPK, LANES), f32)
            _, best, bpos = lax.fori_loop(0, PEER_TOPK, pick, (cand, zeros, zeros))
            k1f = jnp.floor(bpos * (1.0 / PEER_TOPK))
            k2f = bpos - PEER_TOPK * k1f
            e1 = jnp.zeros_like(bpos)
            e2 = jnp.zeros_like(bpos)
            for j in range(PEER_TOPK):
                e1 = e1 + jnp.where(k1f == float(j), i1[j:j + 1, :], 0.0)
                e2 = e2 + jnp.where(k2f == float(j), i2[j:j + 1, :], 0.0)
            expert = e1 * float(n_keys) + e2
            ex = jnp.exp(best - best[0:1, :])
            gate = ex / jnp.sum(ex, axis=0, keepdims=True)
            r0 = pl.multiple_of(h * PEER_TOPK, PEER_TOPK)
            e_scr[pl.ds(r0, PEER_TOPK), :] = expert
            g_scr[pl.ds(r0, PEER_TOPK), :] = gate
            return 0

        lax.fori_loop(0, n_heads, head_body, 0)
        idx_ref[lb * LANES:(lb + 1) * LANES, :] = e_scr[...].T.astype(jnp.int32)
        gate_ref[:, lb * LANES:(lb + 1) * LANES] = g_scr[...]


def _route(h2, w_pq, sub_keys2, n_heads, tm):
    m, d = h2.shape
    nq = w_pq.shape[1]
    n_keys = sub_keys2.shape[1]
    ne = n_heads * PEER_TOPK
    assert ne == LANES and tm % LANES == 0 and m % tm == 0
    cpos = jnp.asarray(np.broadcast_to(_cand_layout()[:, None], (N_CAND, LANES)).copy())
    kern = functools.partial(_route_kernel, n_heads=n_heads, tm=tm)
    return pl.pallas_call(
        kern,
        grid=(m // tm,),
        in_specs=[
            pl.BlockSpec((tm, d), lambda i: (i, 0)),
            _resident((d, nq), lambda i: (0, 0)),
            _resident((2 * n_heads, n_keys, HEAD_DIM), lambda i: (0, 0, 0)),
            pl.BlockSpec((N_CAND, LANES), lambda i: (0, 0)),
        ],
        out_specs=[
            pl.BlockSpec((tm, ne), lambda i: (i, 0)),
            pl.BlockSpec((ne, tm), lambda i: (0, i)),
        ],
        out_shape=[jax.ShapeDtypeStruct((m, ne), jnp.int32), jax.ShapeDtypeStruct((ne, m), f32)],
        scratch_shapes=[
            pltpu.VMEM((2 * n_heads, tm, HEAD_DIM), bf16),
            pltpu.VMEM((ne, LANES), f32),
            pltpu.VMEM((ne, LANES), f32),
        ],
        compiler_params=_params(1),
        name="peer_route",
    )(h2, w_pq, sub_keys2, cpos)


SLAB = 16


def _sublane_sums(r, sub):
    def merge(a, b, half):
        lo = (sub % (2 * half)) < half
        if 2 * half == 8:
            return jnp.where(lo, a, b) + pltpu.roll(jnp.where(lo, b, a), half, 0)
        x = jnp.where(lo, a, pltpu.roll(b, half, 0))
        y = jnp.where(lo, pltpu.roll(a, 8 - half, 0), b)
        return x + y
    a = [merge(r[j], r[j + 4], 4) for j in range(4)]
    b = [merge(a[j], a[j + 2], 2) for j in range(2)]
    return merge(b[0], b[1], 1)


def _expert_kernel(idx_hbm, uv_hbm, gate_ref, h2_ref, x1_ref, gt2_ref, o_ref, idx_smem, *rest, d, ne):
    rings, acts = rest[:RING], rest[RING:2 * RING]
    hs_scr, ys_scr, idx_sem, row_sem = rest[2 * RING:]
    step = pl.program_id(0)
    n_steps = pl.num_programs(0)
    n_tiles = d // LANES
    n_groups = ne // 8
    last_chunk = n_steps * (PEER_STEP // IDX_CHUNK) - 1
    lane = lax.broadcasted_iota(jnp.int32, (ne, PEER_STEP), 1)
    sub = lax.broadcasted_iota(jnp.int32, (8, LANES), 0)

    def idx_copy(chunk, buf):
        c = jnp.minimum(chunk, last_chunk)
        return pltpu.make_async_copy(idx_hbm.at[pl.ds(c * IDX_CHUNK, IDX_CHUNK)], idx_smem.at[buf],
                                     idx_sem.at[buf])

    def row_copy(t, e):
        buf, r = (t // IDX_CHUNK) % 2, t % IDX_CHUNK
        return pltpu.make_async_copy(uv_hbm.at[idx_smem[buf, r, e]], rings[t % RING].at[e],
                                     row_sem.at[t % RING])

    def wait_rows(slot):
        pltpu.make_async_copy(uv_hbm.at[pl.ds(0, ne)], rings[slot], row_sem.at[slot]).wait()

    @pl.when(step == 0)
    def _():
        idx_copy(0, 0).start()
        idx_copy(1, 1).start()
        idx_copy(0, 0).wait()
        idx_copy(1, 1).wait()

        def prime(e, _):
            for t in range(AHEAD):
                row_copy(t, e).start()
            return 0
        lax.fori_loop(0, ne, prime, 0)
        idx_copy(2, 0).start()

    hf = h2_ref[...].astype(f32)
    for rg in range(PEER_STEP // 8):
        for j in range(n_tiles):
            hs_scr[pl.ds(rg * 8 * n_tiles + j, 8, stride=n_tiles), :] = (
                hf[rg * 8:(rg + 1) * 8, j * LANES:(j + 1) * LANES])

    def ring_body(it, _):
        t0 = it * RING
        first_chunk = (step * PEER_STEP + t0) // IDX_CHUNK

        def start_copies(k, lo, hi):
            for e in range(lo, hi):
                row_copy(k + AHEAD, e).start(priority=e % 2)

        def open_token(k):
            if (k + AHEAD) % IDX_CHUNK == 0:
                q = (k + AHEAD) // IDX_CHUNK
                idx_copy(first_chunk + q, q % 2).wait()
                idx_copy(first_chunk + q + 1, 1 - q % 2).start()
            wait_rows(k)
            gcol = jnp.sum(jnp.where(lane == t0 + k, gate_ref[...], 0.0), axis=1, keepdims=True)
            hs = hs_scr[pl.ds(pl.multiple_of((t0 + k) * n_tiles, n_tiles), n_tiles), :]
            return gcol, hs.astype(bf16)

        def dot(k, hs, groups, n_copies):
            cols = []
            for g in groups:
                parts = []
                for j in range(8):
                    p = (rings[k][g * 8 + j, 0:SLAB, :] * hs).astype(f32)
                    parts.append(p[0:8, :] + p[8:16, :])
                cols.append(jnp.sum(_sublane_sums(parts, sub), axis=1, keepdims=True))
                start_copies(k, g * n_copies // n_groups, (g + 1) * n_copies // n_groups)
            return cols

        def activate(k, cols, gcol):
            s = jnp.concatenate(cols, axis=0)
            act = 0.5 * s * (1.0 + lax.erf(s * (2.0 ** -0.5))) * gcol
            acts[k][...] = jnp.broadcast_to(act, (ne, LANES))

        def axpy(k, stage, first_copy):
            accs = [jnp.zeros((SLAB, LANES), f32) for _ in range(4)]
            n_copies = ne - first_copy
            for g in range(n_groups):
                for e in range(g * 8, g * 8 + 8):
                    v = rings[k][e, SLAB:2 * SLAB, :].astype(f32)
                    accs[e % 4] = accs[e % 4] + acts[k][e:e + 1, :] * v
                start_copies(stage, first_copy + g * n_copies // n_groups,
                             first_copy + (g + 1) * n_copies // n_groups)
            ys_scr[pl.ds(pl.multiple_of((t0 + k) * n_tiles, n_tiles), n_tiles), :] = (
                (accs[0] + accs[1]) + (accs[2] + accs[3]))

        half = n_groups // 2
        gcol, hs = open_token(0)
        cols = dot(0, hs, range(n_groups), ne)
        for k in range(1, RING):
            gcol_k, hs = open_token(k)
            cols_k = dot(k, hs, range(half), DOT_COPIES)
            activate(k - 1, cols, gcol)
            cols_k += dot(k, hs, range(half, n_groups), DOT_COPIES)
            axpy(k - 1, k, DOT_COPIES)
            cols, gcol = cols_k, gcol_k
        activate(RING - 1, cols, gcol)
        axpy(RING - 1, RING - 1, ne)
        return 0

    lax.fori_loop(0, PEER_STEP // RING, ring_body, 0)

    for rg in range(PEER_STEP // 8):
        y = jnp.concatenate([ys_scr[pl.ds(rg * 8 * n_tiles + j, 8, stride=n_tiles), :]
                             for j in range(n_tiles)], axis=1)
        o_ref[rg * 8:(rg + 1) * 8, :] = x1_ref[rg * 8:(rg + 1) * 8, :] + gt2_ref[...] * y

    @pl.when(step == n_steps - 1)
    def _():
        for t in range(AHEAD):
            wait_rows(t % RING)
        last_fetch_stage = max(k for k in range(RING) if (k + AHEAD) % IDX_CHUNK == 0)
        idx_copy(last_chunk, 1 - ((last_fetch_stage + AHEAD) // IDX_CHUNK) % 2).wait()


def _experts(idx, gate_t, h2, x1, mod3, uv, seq):
    m, d = h2.shape
    ne = idx.shape[1]
    assert d == SLAB * LANES and uv.shape[1:] == (2 * SLAB, LANES) and ne % 8 == 0
    assert m % PEER_STEP == 0 and seq % PEER_STEP == 0
    assert RING % (2 * IDX_CHUNK) == 0 and PEER_STEP % RING == 0 and AHEAD <= 2 * IDX_CHUNK
    kern = functools.partial(_expert_kernel, d=d, ne=ne)
    return pl.pallas_call(
        kern,
        grid=(m // PEER_STEP,),
        in_specs=[
            pl.BlockSpec(memory_space=pl.ANY),
            pl.BlockSpec(memory_space=pl.ANY),
            pl.BlockSpec((ne, PEER_STEP), lambda i: (0, i)),
            pl.BlockSpec((PEER_STEP, d), lambda i: (i, 0)),
            pl.BlockSpec((PEER_STEP, d), lambda i: (i, 0)),
            pl.BlockSpec((None, 1, d), lambda i: ((i * PEER_STEP) // seq, 0, 5)),
        ],
        out_specs=pl.BlockSpec((PEER_STEP, d), lambda i: (i, 0)),
        out_shape=jax.ShapeDtypeStruct((m, d), f32),
        scratch_shapes=(
            [pltpu.SMEM((2, IDX_CHUNK, ne), jnp.int32)]
            + [pltpu.VMEM((ne, 2 * SLAB, LANES), bf16) for _ in range(RING)]
            + [pltpu.VMEM((ne, LANES), f32) for _ in range(RING)]
            + [pltpu.VMEM((PEER_STEP * SLAB, LANES), f32),
               pltpu.VMEM((PEER_STEP * SLAB, LANES), f32),
               pltpu.SemaphoreType.DMA((2,)),
               pltpu.SemaphoreType.DMA((RING,))]),
        compiler_params=_params(1),
        name="peer_experts",
    )(idx, uv, gate_t, h2, x1, mod3)


def kernel(x, c, ctx, c_ctx, w_ada, b_ada, g_norm1, w_in, q_norm, k_norm, rpb, w_conv,
           g_attn_out, g_conv_out, w_out, g_norm2, w_pq, sub_keys, u_experts, v_experts):
    batch, seq, d = x.shape
    ctx_len = ctx.shape[1]
    depth = w_ada.shape[0]
    n_heads = rpb.shape[1]
    d_attn = n_heads * HEAD_DIM
    d_conv = w_conv.shape[2]
    p_heads = sub_keys.shape[1]
    assert d % LANES == 0 and seq % (ATTN_Q_ROWS * GRID_W) == 0 and seq // GRID_W >= ATTN_K_ROWS
    assert w_in.shape[2] == 3 * d_attn + 3 * d_conv and g_conv_out.shape[2] == LANES
    assert sub_keys.shape[2] == 2 and sub_keys.shape[4] == HEAD_DIM
    tm = 256

    x2d = x.reshape(batch * seq, d)
    ctx2d = ctx.reshape(batch * ctx_len, d)
    mod_rows = -(-(batch + 1) // 8) * 8
    cvec = jnp.zeros((mod_rows, d), f32).at[:batch].set(c).at[batch].set(c_ctx)

    for l in range(depth):
        assert l == depth - 1
        mod = _ada(cvec, w_ada[l], b_ada[l][None, :])
        mod3 = mod.reshape(mod_rows, 1, mod.shape[1])

        w_in_b = w_in[l].astype(bf16)
        g1 = g_norm1[l][None, :]
        hg = jnp.concatenate([jnp.tile(q_norm[l], n_heads) * (HEAD_DIM ** -0.5),
                              jnp.tile(k_norm[l], n_heads)])[None, :]
        proj = _inproj(x2d, mod3, lambda i: (i * tm) // seq, g1, w_in_b, hg, 2 * n_heads, tm)
        hgc = jnp.tile(k_norm[l], n_heads)[None, :]
        kvc = _inproj(ctx2d, mod3, lambda i: batch, g1, w_in_b[:, d_attn:3 * d_attn], hgc,
                      n_heads, tm)

        bias = _attn_bias(rpb[l], seq // GRID_W)
        oattn = _attention(proj, kvc, bias, g_attn_out[l][:, None, :], batch, seq, ctx_len, n_heads)

        x1, h2 = _mix(oattn, proj, x2d, mod3, w_conv[l], g_conv_out[l].reshape(1, d_conv),
                      w_out[l].astype(bf16), g_norm2[l][None, :], seq, tm)

        sk2 = sub_keys[l].reshape(2 * p_heads, sub_keys.shape[3], HEAD_DIM).astype(bf16)
        idx, gate_t = _route(h2, w_pq[l].astype(bf16), sk2, p_heads, tm)

        n_exp = u_experts.shape[1]
        uv = jnp.concatenate([u_experts[l].astype(bf16).reshape(n_exp, SLAB, LANES),
                              v_experts[l].astype(bf16).reshape(n_exp, SLAB, LANES)], axis=1)
        x2d = _experts(idx, gate_t, h2, x1, mod3, uv, seq)
    return x2d.reshape(batch, seq, d)
```

```python
import functools

import numpy as np
import jax
import jax.numpy as jnp
from jax import lax
from jax.experimental import pallas as pl
from jax.experimental.pallas import tpu as pltpu

HEAD_DIM = 128
GRID_W = 64
WIN_H = 8
WIN_W = 16
CONV_W = 3
PEER_TOPK = 16
EPS = 1e-6
NEG_INF = -1e30

V7X_VMEM_BYTES = 64 * 1024 * 1024
VMEM_LIMIT_BYTES = V7X_VMEM_BYTES - 8 * 1024 * 1024
LANES = 128

ATTN_Q_ROWS = 4
ATTN_K_ROWS = ATTN_Q_ROWS + WIN_H
ATTN_HEADS = 4

RING = 8
AHEAD = RING - 2
IDX_CHUNK = 4
PEER_STEP = 128
DOT_COPIES = 72

f32 = jnp.float32
bf16 = jnp.bfloat16


def _params(n_grid_dims):
    return pltpu.CompilerParams(
        dimension_semantics=("arbitrary",) * n_grid_dims,
        vmem_limit_bytes=VMEM_LIMIT_BYTES,
    )


def _rms(x):
    return x * lax.rsqrt(jnp.mean(x * x, axis=-1, keepdims=True) + EPS)


def _resident(shape, index_map):
    return pl.BlockSpec(shape, index_map, pipeline_mode=pl.Buffered(1))


def _ada_kernel(c_ref, w_ref, b_ref, o_ref):
    c = c_ref[...]
    cond = (c * jax.nn.sigmoid(c)).astype(bf16)
    o_ref[...] = jnp.dot(cond, w_ref[...].astype(bf16), preferred_element_type=f32) + b_ref[...]


def _ada(cvec, w, b):
    r, d = cvec.shape
    n = w.shape[1]
    tn = 1024
    assert n % tn == 0
    return pl.pallas_call(
        _ada_kernel,
        grid=(n // tn,),
        in_specs=[
            pl.BlockSpec((r, d), lambda j: (0, 0)),
            pl.BlockSpec((d, tn), lambda j: (0, j)),
            pl.BlockSpec((1, tn), lambda j: (0, j)),
        ],
        out_specs=pl.BlockSpec((r, tn), lambda j: (0, j)),
        out_shape=jax.ShapeDtypeStruct((r, n), f32),
        compiler_params=_params(1),
        name="ada_mod",
    )(cvec, w, b)


def _inproj_kernel(x_ref, sh_ref, sc_ref, g_ref, w_ref, hg_ref, o_ref, *, n_norm_heads, cw):
    x = x_ref[...]
    h = (_rms(x) * g_ref[...]) * (1.0 + sc_ref[...]) + sh_ref[...]
    hb = h.astype(bf16)
    n = o_ref.shape[1]
    for j in range(n // cw):
        acc = jnp.dot(hb, w_ref[:, j * cw:(j + 1) * cw], preferred_element_type=f32)
        for k in range(cw // HEAD_DIM):
            head = (j * cw) // HEAD_DIM + k
            blk = acc[:, k * HEAD_DIM:(k + 1) * HEAD_DIM]
            if head < n_norm_heads:
                blk = _rms(blk) * hg_ref[:, head * HEAD_DIM:(head + 1) * HEAD_DIM]
            c0 = head * HEAD_DIM
            o_ref[:, c0:c0 + HEAD_DIM] = blk.astype(bf16)


def _inproj(x2d, mod3, mod_row, g, w, hg, n_norm_heads, tm):
    m, d = x2d.shape
    n = w.shape[1]
    cw = 512
    assert m % tm == 0 and n % cw == 0
    kern = functools.partial(_inproj_kernel, n_norm_heads=n_norm_heads, cw=cw)
    return pl.pallas_call(
        kern,
        grid=(m // tm,),
        in_specs=[
            pl.BlockSpec((tm, d), lambda i: (i, 0)),
            pl.BlockSpec((None, 1, d), lambda i: (mod_row(i), 0, 0)),
            pl.BlockSpec((None, 1, d), lambda i: (mod_row(i), 0, 1)),
            pl.BlockSpec((1, d), lambda i: (0, 0)),
            _resident((d, n), lambda i: (0, 0)),
            pl.BlockSpec((1, hg.shape[1]), lambda i: (0, 0)),
        ],
        out_specs=pl.BlockSpec((tm, n), lambda i: (i, 0)),
        out_shape=jax.ShapeDtypeStruct((m, n), bf16),
        compiler_params=_params(1),
        name="in_proj",
    )(x2d, mod3, mod3, g, w, hg)


def _attn_bias(rpb, rows):
    kh = min(WIN_H, rows)
    n_heads = rpb.shape[0]
    col = np.arange(GRID_W)
    cs = np.clip(col - WIN_W // 2, 0, GRID_W - WIN_W)
    col_ok = (col[None, :] >= cs[:, None]) & (col[None, :] < cs[:, None] + WIN_W)
    dc = np.clip(col[None, :] - col[:, None], -(WIN_W - 1), WIN_W - 1) + (WIN_W - 1)
    onehot = dc[None] == np.arange(2 * WIN_W - 1)[:, None, None]
    by_dr = jnp.sum(jnp.where(onehot[None, None], rpb.astype(f32)[:, :, :, None, None], 0.0), axis=2)
    by_dr = jnp.where(col_ok[None, None], by_dr, NEG_INF)
    masked = jnp.full((n_heads, GRID_W, GRID_W), NEG_INF, f32)
    tiles = []
    for r0 in (0, ATTN_Q_ROWS, rows - ATTN_Q_ROWS):
        ws = int(np.clip(r0 - kh // 2, 0, rows - ATTN_K_ROWS))
        q_rows = []
        for r in range(r0, r0 + ATTN_Q_ROWS):
            rs = int(np.clip(r - kh // 2, 0, rows - kh))
            blocks = [by_dr[:, kr - r + (WIN_H - 1)] if rs <= kr < rs + kh else masked
                      for kr in range(ws, ws + ATTN_K_ROWS)]
            q_rows.append(jnp.concatenate(blocks, axis=2))
        tiles.append(jnp.concatenate(q_rows, axis=1))
    return jnp.stack(tiles, axis=1)


def _attn_kernel(q_ref, k_ref, v_ref, kc_ref, vc_ref, bias_ref, g_ref, o_ref, *, rows):
    grp = pl.program_id(2)
    ws = jnp.clip(ATTN_Q_ROWS * grp - WIN_H // 2, 0, rows - ATTN_K_ROWS)
    start = pl.multiple_of(ws * GRID_W, GRID_W)
    nk = ATTN_K_ROWS * GRID_W
    nt = (((1,), (1,)), ((), ()))
    for hh in range(ATTN_HEADS):
        cols = slice(hh * HEAD_DIM, (hh + 1) * HEAD_DIM)
        q = q_ref[:, cols]
        kw = k_ref[pl.ds(start, nk), cols]
        vw = v_ref[pl.ds(start, nk), cols]
        s_w = lax.dot_general(q, kw, nt, preferred_element_type=f32) + bias_ref[hh]
        s_c = lax.dot_general(q, kc_ref[:, cols], nt, preferred_element_type=f32)
        m = jnp.maximum(jnp.max(s_w, axis=-1, keepdims=True), jnp.max(s_c, axis=-1, keepdims=True))
        p_w = jnp.exp(s_w - m)
        p_c = jnp.exp(s_c - m)
        denom = jnp.sum(p_w, axis=-1, keepdims=True) + jnp.sum(p_c, axis=-1, keepdims=True)
        o = (jnp.dot(p_w.astype(bf16), vw, preferred_element_type=f32)
             + jnp.dot(p_c.astype(bf16), vc_ref[:, cols], preferred_element_type=f32))
        o = o / denom
        o_ref[:, cols] = (_rms(o) * g_ref[hh]).astype(bf16)


def _attention(proj, kvc, bias, g_out, batch, seq, ctx_len, n_heads):
    rows = seq // GRID_W
    n_grp = rows // ATTN_Q_ROWS
    tq = ATTN_Q_ROWS * GRID_W
    nq_blk = seq // tq

    def cfg(g):
        return jnp.where(g == 0, 0, jnp.where(g == n_grp - 1, 2, 1))

    assert n_heads % ATTN_HEADS == 0
    hb = n_heads // ATTN_HEADS
    hw = ATTN_HEADS * HEAD_DIM
    kern = functools.partial(_attn_kernel, rows=rows)
    return pl.pallas_call(
        kern,
        grid=(batch, hb, n_grp),
        in_specs=[
            pl.BlockSpec((tq, hw), lambda b, h, g: (b * nq_blk + g, h)),
            pl.BlockSpec((seq, hw), lambda b, h, g: (b, hb + h)),
            pl.BlockSpec((seq, hw), lambda b, h, g: (b, 2 * hb + h)),
            pl.BlockSpec((ctx_len, hw), lambda b, h, g: (b, h)),
            pl.BlockSpec((ctx_len, hw), lambda b, h, g: (b, hb + h)),
            pl.BlockSpec((ATTN_HEADS, None, tq, ATTN_K_ROWS * GRID_W),
                         lambda b, h, g: (h, cfg(g), 0, 0)),
            pl.BlockSpec((ATTN_HEADS, 1, HEAD_DIM), lambda b, h, g: (h, 0, 0)),
        ],
        out_specs=pl.BlockSpec((tq, hw), lambda b, h, g: (b * nq_blk + g, h)),
        out_shape=jax.ShapeDtypeStruct((batch * seq, n_heads * HEAD_DIM), bf16),
        compiler_params=_params(3),
        name="nbr_attention",
    )(proj, proj, proj, kvc, kvc, bias, g_out)


HALO = 16


def _mix_kernel(oa_ref, bg_ref, cg_ref, u_ref, cgp_ref, up_ref, cgn_ref, un_ref, x_ref, wc_ref,
                gco_ref, wo_ref, gt1_ref, g2_ref, sh2_ref, sc2_ref, x1_ref, h2_ref, *, tm, seq):
    t0 = (pl.program_id(0) * tm) % seq
    z = cg_ref[...].astype(f32) * u_ref[...].astype(f32)
    zp = cgp_ref[HALO - 1:HALO, :].astype(f32) * up_ref[HALO - 1:HALO, :].astype(f32)
    zn = cgn_ref[0:1, :].astype(f32) * un_ref[0:1, :].astype(f32)
    zp = jnp.where(t0 > 0, zp, 0.0)
    zn = jnp.where(t0 + tm < seq, zn, 0.0)
    row = lax.broadcasted_iota(jnp.int32, (tm, 1), 0)
    z_m1 = jnp.where(row == 0, zp, pltpu.roll(z, 1, 0))
    z_p1 = jnp.where(row == tm - 1, zn, pltpu.roll(z, tm - 1, 0))
    y = wc_ref[0:1, :] * z_m1 + wc_ref[1:2, :] * z + wc_ref[2:3, :] * z_p1
    oc = bg_ref[...].astype(f32) * y
    pieces = []
    for gi in range(oc.shape[1] // LANES):
        blk = oc[:, gi * LANES:(gi + 1) * LANES]
        pieces.append((_rms(blk) * gco_ref[:, gi * LANES:(gi + 1) * LANES]).astype(bf16))
    ocn = jnp.concatenate(pieces, axis=1)
    da = oa_ref.shape[1]
    y2 = (jnp.dot(oa_ref[...], wo_ref[0:da, :], preferred_element_type=f32)
          + jnp.dot(ocn, wo_ref[da:, :], preferred_element_type=f32))
    x1 = x_ref[...] + gt1_ref[...] * y2
    x1_ref[...] = x1
    h2 = (_rms(x1) * g2_ref[...]) * (1.0 + sc2_ref[...]) + sh2_ref[...]
    h2_ref[...] = h2.astype(bf16)


def _mix(oattn, proj, x2d, mod3, w_conv, g_conv, w_out, g2, seq, tm):
    m, d = x2d.shape
    da = oattn.shape[1]
    dc = w_conv.shape[1]
    assert dc % LANES == 0 and da % dc == 0 and tm % HALO == 0 and seq % tm == 0
    cb = (3 * da) // dc
    hb = tm // HALO
    n_halo = m // HALO

    def brow(i):
        return (i * tm) // seq

    kern = functools.partial(_mix_kernel, tm=tm, seq=seq)
    return pl.pallas_call(
        kern,
        grid=(m // tm,),
        in_specs=[
            pl.BlockSpec((tm, da), lambda i: (i, 0)),
            pl.BlockSpec((tm, dc), lambda i: (i, cb)),
            pl.BlockSpec((tm, dc), lambda i: (i, cb + 1)),
            pl.BlockSpec((tm, dc), lambda i: (i, cb + 2)),
            pl.BlockSpec((HALO, dc), lambda i: (jnp.maximum(i * hb - 1, 0), cb + 1)),
            pl.BlockSpec((HALO, dc), lambda i: (jnp.maximum(i * hb - 1, 0), cb + 2)),
            pl.BlockSpec((HALO, dc), lambda i: (jnp.minimum((i + 1) * hb, n_halo - 1), cb + 1)),
            pl.BlockSpec((HALO, dc), lambda i: (jnp.minimum((i + 1) * hb, n_halo - 1), cb + 2)),
            pl.BlockSpec((tm, d), lambda i: (i, 0)),
            pl.BlockSpec((CONV_W, dc), lambda i: (0, 0)),
            pl.BlockSpec((1, dc), lambda i: (0, 0)),
            _resident((da + dc, d), lambda i: (0, 0)),
            pl.BlockSpec((None, 1, d), lambda i: (brow(i), 0, 2)),
            pl.BlockSpec((1, d), lambda i: (0, 0)),
            pl.BlockSpec((None, 1, d), lambda i: (brow(i), 0, 3)),
            pl.BlockSpec((None, 1, d), lambda i: (brow(i), 0, 4)),
        ],
        out_specs=[
            pl.BlockSpec((tm, d), lambda i: (i, 0)),
            pl.BlockSpec((tm, d), lambda i: (i, 0)),
        ],
        out_shape=[jax.ShapeDtypeStruct((m, d), f32), jax.ShapeDtypeStruct((m, d), bf16)],
        compiler_params=_params(1),
        name="conv_out_proj",
    )(oattn, proj, proj, proj, proj, proj, proj, proj, x2d, w_conv, g_conv, w_out,
      mod3, g2, mod3, mod3)


def _cand_layout():
    pos = [k2 for k2 in range(PEER_TOPK)]
    for k1 in range(1, 8):
        pos += [k1 * PEER_TOPK + k2 for k2 in range(8)]
    pos += [k1 * PEER_TOPK for k1 in range(8, PEER_TOPK)]
    return np.asarray(pos, np.float32)


N_CAND = 16 + 7 * 8 + 8


def _top16_desc(s, iota_k, iota_t):
    n_keys = s.shape[0]

    def body(t, carry):
        s, vals, idxs = carry
        m = jnp.max(s, axis=0, keepdims=True)
        idx = jnp.min(jnp.where(s == m, iota_k, float(n_keys)), axis=0, keepdims=True)
        s = jnp.where(iota_k == idx, -jnp.inf, s)
        hit = iota_t == t
        return s, jnp.where(hit, m, vals), jnp.where(hit, idx, idxs)

    zeros = jnp.zeros((PEER_TOPK, s.shape[1]), f32)
    _, vals, idxs = lax.fori_loop(0, PEER_TOPK, body, (s, zeros, zeros))
    return vals, idxs


def _route_kernel(h2_ref, wq_ref, sk_ref, cpos_ref, idx_ref, gate_ref, q_scr, e_scr, g_scr,
                  *, n_heads, tm):
    n_keys = sk_ref.shape[1]
    qf = jnp.dot(h2_ref[...], wq_ref[...], preferred_element_type=f32).astype(bf16)
    for hp in range(2 * n_heads):
        q_scr[hp] = qf[:, hp * HEAD_DIM:(hp + 1) * HEAD_DIM]

    iota_k = lax.broadcasted_iota(jnp.int32, (n_keys, 2 * LANES), 0).astype(f32)
    iota_t2 = lax.broadcasted_iota(jnp.int32, (PEER_TOPK, 2 * LANES), 0)
    iota_t = lax.broadcasted_iota(jnp.int32, (PEER_TOPK, tm), 0)
    cpos = cpos_ref[...]
    nt = (((1,), (1,)), ((), ()))
    n_lb = tm // LANES

    def head_body(h, _):
        cands, i1s, i2s = [], [], []
        for lb in range(n_lb):
            qa = q_scr[2 * h, lb * LANES:(lb + 1) * LANES, :]
            qb = q_scr[2 * h + 1, lb * LANES:(lb + 1) * LANES, :]
            s1 = lax.dot_general(sk_ref[2 * h], qa, nt, preferred_element_type=f32)
            s2 = lax.dot_general(sk_ref[2 * h + 1], qb, nt, preferred_element_type=f32)
            a12, i12 = _top16_desc(jnp.concatenate([s1, s2], axis=1), iota_k, iota_t2)
            a1, a2 = a12[:, :LANES], a12[:, LANES:]
            i1s.append(i12[:, :LANES])
            i2s.append(i12[:, LANES:])
            pieces = [a1[0:1, :] + a2]
            for k1 in range(1, 8):
                pieces.append(a1[k1:k1 + 1, :] + a2[0:8, :])
            pieces.append(a1[8:16, :] + a2[0:1, :])
            cands.append(jnp.concatenate(pieces, axis=0))
        cand = jnp.concatenate(cands, axis=1)
        i1 = jnp.concatenate(i1s, axis=1)
        i2 = jnp.concatenate(i2s, axis=1)

        def pick(t, carry):
            cand, best, bpos = carry
            m = jnp.max(cand, axis=0, keepdims=True)
            p = jnp.min(jnp.where(cand == m, cpos, 1e9), axis=0, keepdims=True)
            cand = jnp.where(cpos == p, -jnp.inf, cand)
            hit = iota_t == t
            return cand, jnp.where(hit, m, best), jnp.where(hit, p, bpos)

        zeros = jnp.zeros((PEER_TOPK, tm), f32)
        _, best, bpos = lax.fori_loop(0, PEER_TOPK, pick, (cand, zeros, zeros))
        k1f = jnp.floor(bpos * (1.0 / PEER_TOPK))
        k2f = bpos - PEER_TOPK * k1f
        e1 = jnp.zeros_like(bpos)
        e2 = jnp.zeros_like(bpos)
        for j in range(PEER_TOPK):
            e1 = e1 + jnp.where(k1f == float(j), i1[j:j + 1, :], 0.0)
            e2 = e2 + jnp.where(k2f == float(j), i2[j:j + 1, :], 0.0)
        expert = e1 * float(n_keys) + e2
        ex = jnp.exp(best - best[0:1, :])
        gate = ex / jnp.sum(ex, axis=0, keepdims=True)
        r0 = pl.multiple_of(h * PEER_TOPK, PEER_TOPK)
        e_scr[pl.ds(r0, PEER_TOPK), :] = expert
        g_scr[pl.ds(r0, PEER_TOPK), :] = gate
        return 0

    lax.fori_loop(0, n_heads, head_body, 0)
    for lb in range(n_lb):
        idx_ref[lb * LANES:(lb + 1) * LANES, :] = (
            e_scr[:, lb * LANES:(lb + 1) * LANES].T.astype(jnp.int32))
    gate_ref[...] = g_scr[...]


def _route(h2, w_pq, sub_keys2, n_heads, tm):
    m, d = h2.shape
    nq = w_pq.shape[1]
    n_keys = sub_keys2.shape[1]
    ne = n_heads * PEER_TOPK
    assert ne == LANES and tm % LANES == 0 and m % tm == 0
    cpos = jnp.asarray(np.broadcast_to(_cand_layout()[:, None], (N_CAND, tm)).copy())
    kern = functools.partial(_route_kernel, n_heads=n_heads, tm=tm)
    return pl.pallas_call(
        kern,
        grid=(m // tm,),
        in_specs=[
            pl.BlockSpec((tm, d), lambda i: (i, 0)),
            _resident((d, nq), lambda i: (0, 0)),
            _resident((2 * n_heads, n_keys, HEAD_DIM), lambda i: (0, 0, 0)),
            pl.BlockSpec((N_CAND, tm), lambda i: (0, 0)),
        ],
        out_specs=[
            pl.BlockSpec((tm, ne), lambda i: (i, 0)),
            pl.BlockSpec((ne, tm), lambda i: (0, i)),
        ],
        out_shape=[jax.ShapeDtypeStruct((m, ne), jnp.int32), jax.ShapeDtypeStruct((ne, m), f32)],
        scratch_shapes=[
            pltpu.VMEM((2 * n_heads, tm, HEAD_DIM), bf16),
            pltpu.VMEM((ne, tm), f32),
            pltpu.VMEM((ne, tm), f32),
        ],
        compiler_params=_params(1),
        name="peer_route",
    )(h2, w_pq, sub_keys2, cpos)


SLAB = 16


def _sublane_sums(r, sub):
    def merge(a, b, half):
        lo = (sub % (2 * half)) < half
        if 2 * half == 8:
            return jnp.where(lo, a, b) + pltpu.roll(jnp.where(lo, b, a), half, 0)
        x = jnp.where(lo, a, pltpu.roll(b, half, 0))
        y = jnp.where(lo, pltpu.roll(a, 8 - half, 0), b)
        return x + y
    a = [merge(r[j], r[j + 4], 4) for j in range(4)]
    b = [merge(a[j], a[j + 2], 2) for j in range(2)]
    return merge(b[0], b[1], 1)


def _expert_kernel(idx_hbm, uv_hbm, gate_ref, h2_ref, x1_ref, gt2_ref, o_ref, idx_smem, *rest, d, ne):
    rings, acts = rest[:RING], rest[RING:2 * RING]
    hs_scr, ys_scr, idx_sem, row_sem = rest[2 * RING:]
    step = pl.program_id(0)
    n_steps = pl.num_programs(0)
    n_tiles = d // LANES
    n_groups = ne // 8
    last_chunk = n_steps * (PEER_STEP // IDX_CHUNK) - 1
    lane = lax.broadcasted_iota(jnp.int32, (ne, PEER_STEP), 1)
    sub = lax.broadcasted_iota(jnp.int32, (8, LANES), 0)

    def idx_copy(chunk, buf):
        c = jnp.minimum(chunk, last_chunk)
        return pltpu.make_async_copy(idx_hbm.at[pl.ds(c * IDX_CHUNK, IDX_CHUNK)], idx_smem.at[buf],
                                     idx_sem.at[buf])

    def row_copy(t, e):
        buf, r = (t // IDX_CHUNK) % 2, t % IDX_CHUNK
        return pltpu.make_async_copy(uv_hbm.at[idx_smem[buf, r, e]], rings[t % RING].at[e],
                                     row_sem.at[t % RING])

    def wait_rows(slot):
        pltpu.make_async_copy(uv_hbm.at[pl.ds(0, ne)], rings[slot], row_sem.at[slot]).wait()

    @pl.when(step == 0)
    def _():
        idx_copy(0, 0).start()
        idx_copy(1, 1).start()
        idx_copy(0, 0).wait()
        idx_copy(1, 1).wait()

        def prime(e, _):
            for t in range(AHEAD):
                row_copy(t, e).start()
            return 0
        lax.fori_loop(0, ne, prime, 0)
        idx_copy(2, 0).start()

    hf = h2_ref[...].astype(f32)
    for rg in range(PEER_STEP // 8):
        for j in range(n_tiles):
            hs_scr[pl.ds(rg * 8 * n_tiles + j, 8, stride=n_tiles), :] = (
                hf[rg * 8:(rg + 1) * 8, j * LANES:(j + 1) * LANES])

    def ring_body(it, _):
        t0 = it * RING
        first_chunk = (step * PEER_STEP + t0) // IDX_CHUNK

        def start_copies(k, lo, hi):
            for e in range(lo, hi):
                row_copy(k + AHEAD, e).start(priority=e % 2)

        def open_token(k):
            if (k + AHEAD) % IDX_CHUNK == 0:
                q = (k + AHEAD) // IDX_CHUNK
                idx_copy(first_chunk + q, q % 2).wait()
                idx_copy(first_chunk + q + 1, 1 - q % 2).start()
            wait_rows(k)
            gcol = jnp.sum(jnp.where(lane == t0 + k, gate_ref[...], 0.0), axis=1, keepdims=True)
            hs = hs_scr[pl.ds(pl.multiple_of((t0 + k) * n_tiles, n_tiles), n_tiles), :]
            return gcol, hs

        def dot(k, hs, groups, n_copies):
            cols = []
            for g in groups:
                parts = []
                for j in range(8):
                    p = rings[k][g * 8 + j, 0:SLAB, :].astype(f32) * hs
                    parts.append(p[0:8, :] + p[8:16, :])
                cols.append(jnp.sum(_sublane_sums(parts, sub), axis=1, keepdims=True))
                start_copies(k, g * n_copies // n_groups, (g + 1) * n_copies // n_groups)
            return cols

        def activate(k, cols, gcol):
            s = jnp.concatenate(cols, axis=0)
            act = 0.5 * s * (1.0 + lax.erf(s * (2.0 ** -0.5))) * gcol
            acts[k][...] = jnp.broadcast_to(act, (ne, LANES))

        def axpy(k, stage, first_copy):
            accs = [jnp.zeros((SLAB, LANES), f32) for _ in range(4)]
            n_copies = ne - first_copy
            for g in range(n_groups):
                for e in range(g * 8, g * 8 + 8):
                    v = rings[k][e, SLAB:2 * SLAB, :].astype(f32)
                    accs[e % 4] = accs[e % 4] + acts[k][e:e + 1, :] * v
                start_copies(stage, first_copy + g * n_copies // n_groups,
                             first_copy + (g + 1) * n_copies // n_groups)
            ys_scr[pl.ds(pl.multiple_of((t0 + k) * n_tiles, n_tiles), n_tiles), :] = (
                (accs[0] + accs[1]) + (accs[2] + accs[3]))

        half = n_groups // 2
        gcol, hs = open_token(0)
        cols = dot(0, hs, range(n_groups), ne)
        for k in range(1, RING):
            gcol_k, hs = open_token(k)
            cols_k = dot(k, hs, range(half), DOT_COPIES)
            activate(k - 1, cols, gcol)
            cols_k += dot(k, hs, range(half, n_groups), DOT_COPIES)
            axpy(k - 1, k, DOT_COPIES)
            cols, gcol = cols_k, gcol_k
        activate(RING - 1, cols, gcol)
        axpy(RING - 1, RING - 1, ne)
        return 0

    lax.fori_loop(0, PEER_STEP // RING, ring_body, 0)

    for rg in range(PEER_STEP // 8):
        y = jnp.concatenate([ys_scr[pl.ds(rg * 8 * n_tiles + j, 8, stride=n_tiles), :]
                             for j in range(n_tiles)], axis=1)
        o_ref[rg * 8:(rg + 1) * 8, :] = x1_ref[rg * 8:(rg + 1) * 8, :] + gt2_ref[...] * y

    @pl.when(step == n_steps - 1)
    def _():
        for t in range(AHEAD):
            wait_rows(t % RING)
        last_fetch_stage = max(k for k in range(RING) if (k + AHEAD) % IDX_CHUNK == 0)
        idx_copy(last_chunk, 1 - ((last_fetch_stage + AHEAD) // IDX_CHUNK) % 2).wait()


def _experts(idx, gate_t, h2, x1, mod3, uv, seq):
    m, d = h2.shape
    ne = idx.shape[1]
    assert d == SLAB * LANES and uv.shape[1:] == (2 * SLAB, LANES) and ne % 8 == 0
    assert m % PEER_STEP == 0 and seq % PEER_STEP == 0
    assert RING % (2 * IDX_CHUNK) == 0 and PEER_STEP % RING == 0 and AHEAD <= 2 * IDX_CHUNK
    kern = functools.partial(_expert_kernel, d=d, ne=ne)
    return pl.pallas_call(
        kern,
        grid=(m // PEER_STEP,),
        in_specs=[
            pl.BlockSpec(memory_space=pl.ANY),
            pl.BlockSpec(memory_space=pl.ANY),
            pl.BlockSpec((ne, PEER_STEP), lambda i: (0, i)),
            pl.BlockSpec((PEER_STEP, d), lambda i: (i, 0)),
            pl.BlockSpec((PEER_STEP, d), lambda i: (i, 0)),
            pl.BlockSpec((None, 1, d), lambda i: ((i * PEER_STEP) // seq, 0, 5)),
        ],
        out_specs=pl.BlockSpec((PEER_STEP, d), lambda i: (i, 0)),
        out_shape=jax.ShapeDtypeStruct((m, d), f32),
        scratch_shapes=(
            [pltpu.SMEM((2, IDX_CHUNK, ne), jnp.int32)]
            + [pltpu.VMEM((ne, 2 * SLAB, LANES), bf16) for _ in range(RING)]
            + [pltpu.VMEM((ne, LANES), f32) for _ in range(RING)]
            + [pltpu.VMEM((PEER_STEP * SLAB, LANES), f32),
               pltpu.VMEM((PEER_STEP * SLAB, LANES), f32),
               pltpu.SemaphoreType.DMA((2,)),
               pltpu.SemaphoreType.DMA((RING,))]),
        compiler_params=_params(1),
        name="peer_experts",
    )(idx, uv, gate_t, h2, x1, mod3)


def kernel(x, c, ctx, c_ctx, w_ada, b_ada, g_norm1, w_in, q_norm, k_norm, rpb, w_conv,
           g_attn_out, g_conv_out, w_out, g_norm2, w_pq, sub_keys, u_experts, v_experts):
    batch, seq, d = x.shape
    ctx_len = ctx.shape[1]
    depth = w_ada.shape[0]
    n_heads = rpb.shape[1]
    d_attn = n_heads * HEAD_DIM
    d_conv = w_conv.shape[2]
    p_heads = sub_keys.shape[1]
    assert d % LANES == 0 and seq % (ATTN_Q_ROWS * GRID_W) == 0 and seq // GRID_W >= ATTN_K_ROWS
    assert w_in.shape[2] == 3 * d_attn + 3 * d_conv and g_conv_out.shape[2] == LANES
    assert sub_keys.shape[2] == 2 and sub_keys.shape[4] == HEAD_DIM
    tm = 256

    x2d = x.reshape(batch * seq, d)
    ctx2d = ctx.reshape(batch * ctx_len, d)
    mod_rows = -(-(batch + 1) // 8) * 8
    cvec = jnp.zeros((mod_rows, d), f32).at[:batch].set(c).at[batch].set(c_ctx)

    for l in range(depth):
        assert l == depth - 1
        mod = _ada(cvec, w_ada[l], b_ada[l][None, :])
        mod3 = mod.reshape(mod_rows, 1, mod.shape[1])

        w_in_b = w_in[l].astype(bf16)
        g1 = g_norm1[l][None, :]
        hg = jnp.concatenate([jnp.tile(q_norm[l], n_heads) * (HEAD_DIM ** -0.5),
                              jnp.tile(k_norm[l], n_heads)])[None, :]
        proj = _inproj(x2d, mod3, lambda i: (i * tm) // seq, g1, w_in_b, hg, 2 * n_heads, tm)
        hgc = jnp.tile(k_norm[l], n_heads)[None, :]
        kvc = _inproj(ctx2d, mod3, lambda i: batch, g1, w_in_b[:, d_attn:3 * d_attn], hgc,
                      n_heads, tm)

        bias = _attn_bias(rpb[l], seq // GRID_W)
        oattn = _attention(proj, kvc, bias, g_attn_out[l][:, None, :], batch, seq, ctx_len, n_heads)

        x1, h2 = _mix(oattn, proj, x2d, mod3, w_conv[l], g_conv_out[l].reshape(1, d_conv),
                      w_out[l].astype(bf16), g_norm2[l][None, :], seq, tm)

        sk2 = sub_keys[l].reshape(2 * p_heads, sub_keys.shape[3], HEAD_DIM).astype(bf16)
        idx, gate_t = _route(h2, w_pq[l].astype(bf16), sk2, p_heads, tm)

        n_exp = u_experts.shape[1]
        uv = jnp.concatenate([u_experts[l].astype(bf16).reshape(n_exp, SLAB, LANES),
                              v_experts[l].astype(bf16).reshape(n_exp, SLAB, LANES)], axis=1)
        x2d = _experts(idx, gate_t, h2, x1, mod3, uv, seq)
    return x2d.reshape(batch, seq, d)
```

```python
import functools

import numpy as np
import jax
import jax.numpy as jnp
from jax import lax
from jax.experimental import pallas as pl
from jax.experimental.pallas import tpu as pltpu

HEAD_DIM = 128
GRID_W = 64
WIN_H = 8
WIN_W = 16
CONV_W = 3
PEER_TOPK = 16
EPS = 1e-6
NEG_INF = -1e30

V7X_VMEM_BYTES = 64 * 1024 * 1024
VMEM_LIMIT_BYTES = V7X_VMEM_BYTES - 8 * 1024 * 1024
LANES = 128

ATTN_Q_ROWS = 4
ATTN_K_ROWS = ATTN_Q_ROWS + WIN_H
ATTN_HEADS = 4

RING = 8
AHEAD = RING - 2
IDX_CHUNK = 4
PEER_STEP = 128
DOT_COPIES = 72

f32 = jnp.float32
bf16 = jnp.bfloat16


def _params(n_grid_dims):
    return pltpu.CompilerParams(
        dimension_semantics=("arbitrary",) * n_grid_dims,
        vmem_limit_bytes=VMEM_LIMIT_BYTES,
    )


def _rms(x):
    return x * lax.rsqrt(jnp.mean(x * x, axis=-1, keepdims=True) + EPS)


def _resident(shape, index_map):
    return pl.BlockSpec(shape, index_map, pipeline_mode=pl.Buffered(1))


def _ada_kernel(c_ref, w_ref, b_ref, o_ref):
    c = c_ref[...]
    cond = (c * jax.nn.sigmoid(c)).astype(bf16)
    o_ref[...] = jnp.dot(cond, w_ref[...].astype(bf16), preferred_element_type=f32) + b_ref[...]


def _ada(cvec, w, b):
    r, d = cvec.shape
    n = w.shape[1]
    tn = 1024
    assert n % tn == 0
    return pl.pallas_call(
        _ada_kernel,
        grid=(n // tn,),
        in_specs=[
            pl.BlockSpec((r, d), lambda j: (0, 0)),
            pl.BlockSpec((d, tn), lambda j: (0, j)),
            pl.BlockSpec((1, tn), lambda j: (0, j)),
        ],
        out_specs=pl.BlockSpec((r, tn), lambda j: (0, j)),
        out_shape=jax.ShapeDtypeStruct((r, n), f32),
        compiler_params=_params(1),
        name="ada_mod",
    )(cvec, w, b)


def _inproj_kernel(x_ref, sh_ref, sc_ref, g_ref, w_ref, hg_ref, o_ref, *, n_norm_heads, cw):
    x = x_ref[...]
    h = (_rms(x) * g_ref[...]) * (1.0 + sc_ref[...]) + sh_ref[...]
    hb = h.astype(bf16)
    n = o_ref.shape[1]
    for j in range(n // cw):
        acc = jnp.dot(hb, w_ref[:, j * cw:(j + 1) * cw], preferred_element_type=f32)
        for k in range(cw // HEAD_DIM):
            head = (j * cw) // HEAD_DIM + k
            blk = acc[:, k * HEAD_DIM:(k + 1) * HEAD_DIM]
            if head < n_norm_heads:
                blk = _rms(blk) * hg_ref[:, head * HEAD_DIM:(head + 1) * HEAD_DIM]
            c0 = head * HEAD_DIM
            o_ref[:, c0:c0 + HEAD_DIM] = blk.astype(bf16)


def _inproj(x2d, mod3, mod_row, g, w, hg, n_norm_heads, tm):
    m, d = x2d.shape
    n = w.shape[1]
    cw = 512
    assert m % tm == 0 and n % cw == 0
    kern = functools.partial(_inproj_kernel, n_norm_heads=n_norm_heads, cw=cw)
    return pl.pallas_call(
        kern,
        grid=(m // tm,),
        in_specs=[
            pl.BlockSpec((tm, d), lambda i: (i, 0)),
            pl.BlockSpec((None, 1, d), lambda i: (mod_row(i), 0, 0)),
            pl.BlockSpec((None, 1, d), lambda i: (mod_row(i), 0, 1)),
            pl.BlockSpec((1, d), lambda i: (0, 0)),
            _resident((d, n), lambda i: (0, 0)),
            pl.BlockSpec((1, hg.shape[1]), lambda i: (0, 0)),
        ],
        out_specs=pl.BlockSpec((tm, n), lambda i: (i, 0)),
        out_shape=jax.ShapeDtypeStruct((m, n), bf16),
        compiler_params=_params(1),
        name="in_proj",
    )(x2d, mod3, mod3, g, w, hg)


def _attn_bias(rpb, rows):
    kh = min(WIN_H, rows)
    n_heads = rpb.shape[0]
    col = np.arange(GRID_W)
    cs = np.clip(col - WIN_W // 2, 0, GRID_W - WIN_W)
    col_ok = (col[None, :] >= cs[:, None]) & (col[None, :] < cs[:, None] + WIN_W)
    dc = np.clip(col[None, :] - col[:, None], -(WIN_W - 1), WIN_W - 1) + (WIN_W - 1)
    onehot = dc[None] == np.arange(2 * WIN_W - 1)[:, None, None]
    by_dr = jnp.sum(jnp.where(onehot[None, None], rpb.astype(f32)[:, :, :, None, None], 0.0), axis=2)
    by_dr = jnp.where(col_ok[None, None], by_dr, NEG_INF)
    masked = jnp.full((n_heads, GRID_W, GRID_W), NEG_INF, f32)
    tiles = []
    for r0 in (0, ATTN_Q_ROWS, rows - ATTN_Q_ROWS):
        ws = int(np.clip(r0 - kh // 2, 0, rows - ATTN_K_ROWS))
        q_rows = []
        for r in range(r0, r0 + ATTN_Q_ROWS):
            rs = int(np.clip(r - kh // 2, 0, rows - kh))
            blocks = [by_dr[:, kr - r + (WIN_H - 1)] if rs <= kr < rs + kh else masked
                      for kr in range(ws, ws + ATTN_K_ROWS)]
            q_rows.append(jnp.concatenate(blocks, axis=2))
        tiles.append(jnp.concatenate(q_rows, axis=1))
    return jnp.stack(tiles, axis=1)


def _attn_kernel(q_ref, k_ref, v_ref, kc_ref, vc_ref, bias_ref, g_ref, o_ref, *, rows):
    grp = pl.program_id(2)
    ws = jnp.clip(ATTN_Q_ROWS * grp - WIN_H // 2, 0, rows - ATTN_K_ROWS)
    start = pl.multiple_of(ws * GRID_W, GRID_W)
    nk = ATTN_K_ROWS * GRID_W
    nt = (((1,), (1,)), ((), ()))
    for hh in range(ATTN_HEADS):
        cols = slice(hh * HEAD_DIM, (hh + 1) * HEAD_DIM)
        q = q_ref[:, cols]
        kw = k_ref[pl.ds(start, nk), cols]
        vw = v_ref[pl.ds(start, nk), cols]
        s_w = lax.dot_general(q, kw, nt, preferred_element_type=f32) + bias_ref[hh]
        s_c = lax.dot_general(q, kc_ref[:, cols], nt, preferred_element_type=f32)
        m = jnp.maximum(jnp.max(s_w, axis=-1, keepdims=True), jnp.max(s_c, axis=-1, keepdims=True))
        p_w = jnp.exp(s_w - m)
        p_c = jnp.exp(s_c - m)
        denom = jnp.sum(p_w, axis=-1, keepdims=True) + jnp.sum(p_c, axis=-1, keepdims=True)
        o = (jnp.dot(p_w.astype(bf16), vw, preferred_element_type=f32)
             + jnp.dot(p_c.astype(bf16), vc_ref[:, cols], preferred_element_type=f32))
        o = o / denom
        o_ref[:, cols] = (_rms(o) * g_ref[hh]).astype(bf16)


def _attention(proj, kvc, bias, g_out, batch, seq, ctx_len, n_heads):
    rows = seq // GRID_W
    n_grp = rows // ATTN_Q_ROWS
    tq = ATTN_Q_ROWS * GRID_W
    nq_blk = seq // tq

    def cfg(g):
        return jnp.where(g == 0, 0, jnp.where(g == n_grp - 1, 2, 1))

    assert n_heads % ATTN_HEADS == 0
    hb = n_heads // ATTN_HEADS
    hw = ATTN_HEADS * HEAD_DIM
    kern = functools.partial(_attn_kernel, rows=rows)
    return pl.pallas_call(
        kern,
        grid=(batch, hb, n_grp),
        in_specs=[
            pl.BlockSpec((tq, hw), lambda b, h, g: (b * nq_blk + g, h)),
            pl.BlockSpec((seq, hw), lambda b, h, g: (b, hb + h)),
            pl.BlockSpec((seq, hw), lambda b, h, g: (b, 2 * hb + h)),
            pl.BlockSpec((ctx_len, hw), lambda b, h, g: (b, h)),
            pl.BlockSpec((ctx_len, hw), lambda b, h, g: (b, hb + h)),
            pl.BlockSpec((ATTN_HEADS, None, tq, ATTN_K_ROWS * GRID_W),
                         lambda b, h, g: (h, cfg(g), 0, 0)),
            pl.BlockSpec((ATTN_HEADS, 1, HEAD_DIM), lambda b, h, g: (h, 0, 0)),
        ],
        out_specs=pl.BlockSpec((tq, hw), lambda b, h, g: (b * nq_blk + g, h)),
        out_shape=jax.ShapeDtypeStruct((batch * seq, n_heads * HEAD_DIM), bf16),
        compiler_params=_params(3),
        name="nbr_attention",
    )(proj, proj, proj, kvc, kvc, bias, g_out)


HALO = 16


def _mix_kernel(oa_ref, bg_ref, cg_ref, u_ref, cgp_ref, up_ref, cgn_ref, un_ref, x_ref, wc_ref,
                gco_ref, wo_ref, gt1_ref, g2_ref, sh2_ref, sc2_ref, x1_ref, h2_ref, *, tm, seq):
    t0 = (pl.program_id(0) * tm) % seq
    z = cg_ref[...].astype(f32) * u_ref[...].astype(f32)
    zp = cgp_ref[HALO - 1:HALO, :].astype(f32) * up_ref[HALO - 1:HALO, :].astype(f32)
    zn = cgn_ref[0:1, :].astype(f32) * un_ref[0:1, :].astype(f32)
    zp = jnp.where(t0 > 0, zp, 0.0)
    zn = jnp.where(t0 + tm < seq, zn, 0.0)
    row = lax.broadcasted_iota(jnp.int32, (tm, 1), 0)
    z_m1 = jnp.where(row == 0, zp, pltpu.roll(z, 1, 0))
    z_p1 = jnp.where(row == tm - 1, zn, pltpu.roll(z, tm - 1, 0))
    y = wc_ref[0:1, :] * z_m1 + wc_ref[1:2, :] * z + wc_ref[2:3, :] * z_p1
    oc = bg_ref[...].astype(f32) * y
    pieces = []
    for gi in range(oc.shape[1] // LANES):
        blk = oc[:, gi * LANES:(gi + 1) * LANES]
        pieces.append((_rms(blk) * gco_ref[:, gi * LANES:(gi + 1) * LANES]).astype(bf16))
    ocn = jnp.concatenate(pieces, axis=1)
    da = oa_ref.shape[1]
    y2 = (jnp.dot(oa_ref[...], wo_ref[0:da, :], preferred_element_type=f32)
          + jnp.dot(ocn, wo_ref[da:, :], preferred_element_type=f32))
    x1 = x_ref[...] + gt1_ref[...] * y2
    x1_ref[...] = x1
    h2 = (_rms(x1) * g2_ref[...]) * (1.0 + sc2_ref[...]) + sh2_ref[...]
    h2_ref[...] = h2.astype(bf16)


def _mix(oattn, proj, x2d, mod3, w_conv, g_conv, w_out, g2, seq, tm):
    m, d = x2d.shape
    da = oattn.shape[1]
    dc = w_conv.shape[1]
    assert dc % LANES == 0 and da % dc == 0 and tm % HALO == 0 and seq % tm == 0
    cb = (3 * da) // dc
    hb = tm // HALO
    n_halo = m // HALO

    def brow(i):
        return (i * tm) // seq

    kern = functools.partial(_mix_kernel, tm=tm, seq=seq)
    return pl.pallas_call(
        kern,
        grid=(m // tm,),
        in_specs=[
            pl.BlockSpec((tm, da), lambda i: (i, 0)),
            pl.BlockSpec((tm, dc), lambda i: (i, cb)),
            pl.BlockSpec((tm, dc), lambda i: (i, cb + 1)),
            pl.BlockSpec((tm, dc), lambda i: (i, cb + 2)),
            pl.BlockSpec((HALO, dc), lambda i: (jnp.maximum(i * hb - 1, 0), cb + 1)),
            pl.BlockSpec((HALO, dc), lambda i: (jnp.maximum(i * hb - 1, 0), cb + 2)),
            pl.BlockSpec((HALO, dc), lambda i: (jnp.minimum((i + 1) * hb, n_halo - 1), cb + 1)),
            pl.BlockSpec((HALO, dc), lambda i: (jnp.minimum((i + 1) * hb, n_halo - 1), cb + 2)),
            pl.BlockSpec((tm, d), lambda i: (i, 0)),
            pl.BlockSpec((CONV_W, dc), lambda i: (0, 0)),
            pl.BlockSpec((1, dc), lambda i: (0, 0)),
            _resident((da + dc, d), lambda i: (0, 0)),
            pl.BlockSpec((None, 1, d), lambda i: (brow(i), 0, 2)),
            pl.BlockSpec((1, d), lambda i: (0, 0)),
            pl.BlockSpec((None, 1, d), lambda i: (brow(i), 0, 3)),
            pl.BlockSpec((None, 1, d), lambda i: (brow(i), 0, 4)),
        ],
        out_specs=[
            pl.BlockSpec((tm, d), lambda i: (i, 0)),
            pl.BlockSpec((tm, d), lambda i: (i, 0)),
        ],
        out_shape=[jax.ShapeDtypeStruct((m, d), f32), jax.ShapeDtypeStruct((m, d), bf16)],
        compiler_params=_params(1),
        name="conv_out_proj",
    )(oattn, proj, proj, proj, proj, proj, proj, proj, x2d, w_conv, g_conv, w_out,
      mod3, g2, mod3, mod3)


def _cand_layout():
    pos = [k2 for k2 in range(PEER_TOPK)]
    for k1 in range(1, 8):
        pos += [k1 * PEER_TOPK + k2 for k2 in range(8)]
    pos += [k1 * PEER_TOPK for k1 in range(8, PEER_TOPK)]
    return np.asarray(pos, np.float32)


N_CAND = 16 + 7 * 8 + 8


def _top16_desc(s, iota_k, iota_t):
    n_keys = s.shape[0]

    def body(t, carry):
        s, vals, idxs = carry
        m = jnp.max(s, axis=0, keepdims=True)
        idx = jnp.min(jnp.where(s == m, iota_k, float(n_keys)), axis=0, keepdims=True)
        s = jnp.where(iota_k == idx, -jnp.inf, s)
        hit = iota_t == t
        return s, jnp.where(hit, m, vals), jnp.where(hit, idx, idxs)

    zeros = jnp.zeros((PEER_TOPK, s.shape[1]), f32)
    _, vals, idxs = lax.fori_loop(0, PEER_TOPK, body, (s, zeros, zeros))
    return vals, idxs


def _route_kernel(h2_ref, wq_ref, sk_ref, cpos_ref, idx_ref, gate_ref, q_scr, e_scr, g_scr,
                  *, n_heads, tm):
    n_keys = sk_ref.shape[1]
    qf = jnp.dot(h2_ref[...], wq_ref[...], preferred_element_type=f32).astype(bf16)
    for hp in range(2 * n_heads):
        q_scr[hp] = qf[:, hp * HEAD_DIM:(hp + 1) * HEAD_DIM]

    iota_k = lax.broadcasted_iota(jnp.int32, (n_keys, 2 * LANES), 0).astype(f32)
    iota_t2 = lax.broadcasted_iota(jnp.int32, (PEER_TOPK, 2 * LANES), 0)
    iota_t = lax.broadcasted_iota(jnp.int32, (PEER_TOPK, tm), 0)
    cpos = cpos_ref[...]
    nt = (((1,), (1,)), ((), ()))
    n_lb = tm // LANES

    def head_body(h, _):
        cands, i1s, i2s = [], [], []
        for lb in range(n_lb):
            qa = q_scr[2 * h, lb * LANES:(lb + 1) * LANES, :]
            qb = q_scr[2 * h + 1, lb * LANES:(lb + 1) * LANES, :]
            s1 = lax.dot_general(sk_ref[2 * h], qa, nt, preferred_element_type=f32)
            s2 = lax.dot_general(sk_ref[2 * h + 1], qb, nt, preferred_element_type=f32)
            a12, i12 = _top16_desc(jnp.concatenate([s1, s2], axis=1), iota_k, iota_t2)
            a1, a2 = a12[:, :LANES], a12[:, LANES:]
            i1s.append(i12[:, :LANES])
            i2s.append(i12[:, LANES:])
            pieces = [a1[0:1, :] + a2]
            for k1 in range(1, 8):
                pieces.append(a1[k1:k1 + 1, :] + a2[0:8, :])
            pieces.append(a1[8:16, :] + a2[0:1, :])
            cands.append(jnp.concatenate(pieces, axis=0))
        cand = jnp.concatenate(cands, axis=1)
        i1 = jnp.concatenate(i1s, axis=1)
        i2 = jnp.concatenate(i2s, axis=1)

        def pick(t, carry):
            cand, best, bpos = carry
            m = jnp.max(cand, axis=0, keepdims=True)
            p = jnp.min(jnp.where(cand == m, cpos, 1e9), axis=0, keepdims=True)
            cand = jnp.where(cpos == p, -jnp.inf, cand)
            hit = iota_t == t
            return cand, jnp.where(hit, m, best), jnp.where(hit, p, bpos)

        zeros = jnp.zeros((PEER_TOPK, tm), f32)
        _, best, bpos = lax.fori_loop(0, PEER_TOPK, pick, (cand, zeros, zeros))
        k1f = jnp.floor(bpos * (1.0 / PEER_TOPK))
        k2f = bpos - PEER_TOPK * k1f
        e1 = jnp.zeros_like(bpos)
        e2 = jnp.zeros_like(bpos)
        for j in range(PEER_TOPK):
            e1 = e1 + jnp.where(k1f == float(j), i1[j:j + 1, :], 0.0)
            e2 = e2 + jnp.where(k2f == float(j), i2[j:j + 1, :], 0.0)
        expert = e1 * float(n_keys) + e2
        ex = jnp.exp(best - best[0:1, :])
        gate = ex / jnp.sum(ex, axis=0, keepdims=True)
        r0 = pl.multiple_of(h * PEER_TOPK, PEER_TOPK)
        e_scr[pl.ds(r0, PEER_TOPK), :] = expert
        g_scr[pl.ds(r0, PEER_TOPK), :] = gate
        return 0

    lax.fori_loop(0, n_heads, head_body, 0)
    for lb in range(n_lb):
        idx_ref[lb * LANES:(lb + 1) * LANES, :] = (
            e_scr[:, lb * LANES:(lb + 1) * LANES].T.astype(jnp.int32))
    gate_ref[...] = g_scr[...]


def _route(h2, w_pq, sub_keys2, n_heads, tm):
    m, d = h2.shape
    nq = w_pq.shape[1]
    n_keys = sub_keys2.shape[1]
    ne = n_heads * PEER_TOPK
    assert ne == LANES and tm % LANES == 0 and m % tm == 0
    cpos = jnp.asarray(np.broadcast_to(_cand_layout()[:, None], (N_CAND, tm)).copy())
    kern = functools.partial(_route_kernel, n_heads=n_heads, tm=tm)
    return pl.pallas_call(
        kern,
        grid=(m // tm,),
        in_specs=[
            pl.BlockSpec((tm, d), lambda i: (i, 0)),
            _resident((d, nq), lambda i: (0, 0)),
            _resident((2 * n_heads, n_keys, HEAD_DIM), lambda i: (0, 0, 0)),
            pl.BlockSpec((N_CAND, tm), lambda i: (0, 0)),
        ],
        out_specs=[
            pl.BlockSpec((tm, ne), lambda i: (i, 0)),
            pl.BlockSpec((ne, tm), lambda i: (0, i)),
        ],
        out_shape=[jax.ShapeDtypeStruct((m, ne), jnp.int32), jax.ShapeDtypeStruct((ne, m), f32)],
        scratch_shapes=[
            pltpu.VMEM((2 * n_heads, tm, HEAD_DIM), bf16),
            pltpu.VMEM((ne, tm), f32),
            pltpu.VMEM((ne, tm), f32),
        ],
        compiler_params=_params(1),
        name="peer_route",
    )(h2, w_pq, sub_keys2, cpos)


SLAB = 16


def _pack_kernel(u_ref, v_ref, o_ref, scr):
    te = u_ref.shape[0]
    n_tiles = u_ref.shape[1] // LANES
    for half, src in enumerate((u_ref, v_ref)):
        for rg in range(te // 8):
            for j in range(n_tiles):
                scr[pl.ds(rg * 8 * n_tiles + j, 8, stride=n_tiles), :] = (
                    src[rg * 8:(rg + 1) * 8, j * LANES:(j + 1) * LANES])
        o_ref[:, half * n_tiles:(half + 1) * n_tiles, :] = (
            scr[...].reshape(te, n_tiles, LANES).astype(bf16))


def _pack_tables(u, v):
    n_exp, d = u.shape
    te = 256
    assert n_exp % te == 0 and d == SLAB * LANES
    return pl.pallas_call(
        _pack_kernel,
        grid=(n_exp // te,),
        in_specs=[pl.BlockSpec((te, d), lambda i: (i, 0)), pl.BlockSpec((te, d), lambda i: (i, 0))],
        out_specs=pl.BlockSpec((te, 2 * SLAB, LANES), lambda i: (i, 0, 0)),
        out_shape=jax.ShapeDtypeStruct((n_exp, 2 * SLAB, LANES), bf16),
        scratch_shapes=[pltpu.VMEM((te * SLAB, LANES), f32)],
        compiler_params=_params(1),
        name="pack_tables",
    )(u, v)


def _sublane_sums(r, sub):
    def merge(a, b, half):
        lo = (sub % (2 * half)) < half
        if 2 * half == 8:
            return jnp.where(lo, a, b) + pltpu.roll(jnp.where(lo, b, a), half, 0)
        x = jnp.where(lo, a, pltpu.roll(b, half, 0))
        y = jnp.where(lo, pltpu.roll(a, 8 - half, 0), b)
        return x + y
    a = [merge(r[j], r[j + 4], 4) for j in range(4)]
    b = [merge(a[j], a[j + 2], 2) for j in range(2)]
    return merge(b[0], b[1], 1)


def _expert_kernel(idx_hbm, uv_hbm, gate_ref, h2_ref, x1_ref, gt2_ref, o_ref, idx_smem, *rest, d, ne):
    rings, acts = rest[:RING], rest[RING:2 * RING]
    hs_scr, ys_scr, idx_sem, row_sem = rest[2 * RING:]
    step = pl.program_id(0)
    n_steps = pl.num_programs(0)
    n_tiles = d // LANES
    n_groups = ne // 8
    last_chunk = n_steps * (PEER_STEP // IDX_CHUNK) - 1
    lane = lax.broadcasted_iota(jnp.int32, (ne, PEER_STEP), 1)
    sub = lax.broadcasted_iota(jnp.int32, (8, LANES), 0)

    def idx_copy(chunk, buf):
        c = jnp.minimum(chunk, last_chunk)
        return pltpu.make_async_copy(idx_hbm.at[pl.ds(c * IDX_CHUNK, IDX_CHUNK)], idx_smem.at[buf],
                                     idx_sem.at[buf])

    def row_copy(t, e):
        buf, r = (t // IDX_CHUNK) % 2, t % IDX_CHUNK
        return pltpu.make_async_copy(uv_hbm.at[idx_smem[buf, r, e]], rings[t % RING].at[e],
                                     row_sem.at[t % RING])

    def wait_rows(slot):
        pltpu.make_async_copy(uv_hbm.at[pl.ds(0, ne)], rings[slot], row_sem.at[slot]).wait()

    @pl.when(step == 0)
    def _():
        idx_copy(0, 0).start()
        idx_copy(1, 1).start()
        idx_copy(0, 0).wait()
        idx_copy(1, 1).wait()

        def prime(e, _):
            for t in range(AHEAD):
                row_copy(t, e).start()
            return 0
        lax.fori_loop(0, ne, prime, 0)
        idx_copy(2, 0).start()

    hf = h2_ref[...].astype(f32)
    for rg in range(PEER_STEP // 8):
        for j in range(n_tiles):
            hs_scr[pl.ds(rg * 8 * n_tiles + j, 8, stride=n_tiles), :] = (
                hf[rg * 8:(rg + 1) * 8, j * LANES:(j + 1) * LANES])

    def ring_body(it, _):
        t0 = it * RING
        first_chunk = (step * PEER_STEP + t0) // IDX_CHUNK

        def start_copies(k, lo, hi):
            for e in range(lo, hi):
                row_copy(k + AHEAD, e).start(priority=e % 2)

        def open_token(k):
            if (k + AHEAD) % IDX_CHUNK == 0:
                q = (k + AHEAD) // IDX_CHUNK
                idx_copy(first_chunk + q, q % 2).wait()
                idx_copy(first_chunk + q + 1, 1 - q % 2).start()
            wait_rows(k)
            gcol = jnp.sum(jnp.where(lane == t0 + k, gate_ref[...], 0.0), axis=1, keepdims=True)
            hs = hs_scr[pl.ds(pl.multiple_of((t0 + k) * n_tiles, n_tiles), n_tiles), :]
            return gcol, hs

        def dot(k, hs, groups, n_copies):
            cols = []
            for g in groups:
                parts = []
                for j in range(8):
                    p = rings[k][g * 8 + j, 0:SLAB, :].astype(f32) * hs
                    parts.append(p[0:8, :] + p[8:16, :])
                cols.append(jnp.sum(_sublane_sums(parts, sub), axis=1, keepdims=True))
                start_copies(k, g * n_copies // n_groups, (g + 1) * n_copies // n_groups)
            return cols

        def activate(k, cols, gcol):
            s = jnp.concatenate(cols, axis=0)
            act = 0.5 * s * (1.0 + lax.erf(s * (2.0 ** -0.5))) * gcol
            acts[k][...] = jnp.broadcast_to(act, (ne, LANES))

        def axpy(k, stage, first_copy):
            accs = [jnp.zeros((SLAB, LANES), f32) for _ in range(4)]
            n_copies = ne - first_copy
            for g in range(n_groups):
                for e in range(g * 8, g * 8 + 8):
                    v = rings[k][e, SLAB:2 * SLAB, :].astype(f32)
                    accs[e % 4] = accs[e % 4] + acts[k][e:e + 1, :] * v
                start_copies(stage, first_copy + g * n_copies // n_groups,
                             first_copy + (g + 1) * n_copies // n_groups)
            ys_scr[pl.ds(pl.multiple_of((t0 + k) * n_tiles, n_tiles), n_tiles), :] = (
                (accs[0] + accs[1]) + (accs[2] + accs[3]))

        half = n_groups // 2
        gcol, hs = open_token(0)
        cols = dot(0, hs, range(n_groups), ne)
        for k in range(1, RING):
            gcol_k, hs = open_token(k)
            cols_k = dot(k, hs, range(half), DOT_COPIES)
            activate(k - 1, cols, gcol)
            cols_k += dot(k, hs, range(half, n_groups), DOT_COPIES)
            axpy(k - 1, k, DOT_COPIES)
            cols, gcol = cols_k, gcol_k
        activate(RING - 1, cols, gcol)
        axpy(RING - 1, RING - 1, ne)
        return 0

    lax.fori_loop(0, PEER_STEP // RING, ring_body, 0)

    for rg in range(PEER_STEP // 8):
        y = jnp.concatenate([ys_scr[pl.ds(rg * 8 * n_tiles + j, 8, stride=n_tiles), :]
                             for j in range(n_tiles)], axis=1)
        o_ref[rg * 8:(rg + 1) * 8, :] = x1_ref[rg * 8:(rg + 1) * 8, :] + gt2_ref[...] * y

    @pl.when(step == n_steps - 1)
    def _():
        for t in range(AHEAD):
            wait_rows(t % RING)
        last_fetch_stage = max(k for k in range(RING) if (k + AHEAD) % IDX_CHUNK == 0)
        idx_copy(last_chunk, 1 - ((last_fetch_stage + AHEAD) // IDX_CHUNK) % 2).wait()


def _experts(idx, gate_t, h2, x1, mod3, uv, seq):
    m, d = h2.shape
    ne = idx.shape[1]
    assert d == SLAB * LANES and uv.shape[1:] == (2 * SLAB, LANES) and ne % 8 == 0
    assert m % PEER_STEP == 0 and seq % PEER_STEP == 0
    assert RING % (2 * IDX_CHUNK) == 0 and PEER_STEP % RING == 0 and AHEAD <= 2 * IDX_CHUNK
    kern = functools.partial(_expert_kernel, d=d, ne=ne)
    return pl.pallas_call(
        kern,
        grid=(m // PEER_STEP,),
        in_specs=[
            pl.BlockSpec(memory_space=pl.ANY),
            pl.BlockSpec(memory_space=pl.ANY),
            pl.BlockSpec((ne, PEER_STEP), lambda i: (0, i)),
            pl.BlockSpec((PEER_STEP, d), lambda i: (i, 0)),
            pl.BlockSpec((PEER_STEP, d), lambda i: (i, 0)),
            pl.BlockSpec((None, 1, d), lambda i: ((i * PEER_STEP) // seq, 0, 5)),
        ],
        out_specs=pl.BlockSpec((PEER_STEP, d), lambda i: (i, 0)),
        out_shape=jax.ShapeDtypeStruct((m, d), f32),
        scratch_shapes=(
            [pltpu.SMEM((2, IDX_CHUNK, ne), jnp.int32)]
            + [pltpu.VMEM((ne, 2 * SLAB, LANES), bf16) for _ in range(RING)]
            + [pltpu.VMEM((ne, LANES), f32) for _ in range(RING)]
            + [pltpu.VMEM((PEER_STEP * SLAB, LANES), f32),
               pltpu.VMEM((PEER_STEP * SLAB, LANES), f32),
               pltpu.SemaphoreType.DMA((2,)),
               pltpu.SemaphoreType.DMA((RING,))]),
        compiler_params=_params(1),
        name="peer_experts",
    )(idx, uv, gate_t, h2, x1, mod3)


def kernel(x, c, ctx, c_ctx, w_ada, b_ada, g_norm1, w_in, q_norm, k_norm, rpb, w_conv,
           g_attn_out, g_conv_out, w_out, g_norm2, w_pq, sub_keys, u_experts, v_experts):
    batch, seq, d = x.shape
    ctx_len = ctx.shape[1]
    depth = w_ada.shape[0]
    n_heads = rpb.shape[1]
    d_attn = n_heads * HEAD_DIM
    d_conv = w_conv.shape[2]
    p_heads = sub_keys.shape[1]
    assert d % LANES == 0 and seq % (ATTN_Q_ROWS * GRID_W) == 0 and seq // GRID_W >= ATTN_K_ROWS
    assert w_in.shape[2] == 3 * d_attn + 3 * d_conv and g_conv_out.shape[2] == LANES
    assert sub_keys.shape[2] == 2 and sub_keys.shape[4] == HEAD_DIM
    tm = 256

    x2d = x.reshape(batch * seq, d)
    ctx2d = ctx.reshape(batch * ctx_len, d)
    mod_rows = -(-(batch + 1) // 8) * 8
    cvec = jnp.zeros((mod_rows, d), f32).at[:batch].set(c).at[batch].set(c_ctx)

    for l in range(depth):
        assert l == depth - 1
        mod = _ada(cvec, w_ada[l], b_ada[l][None, :])
        mod3 = mod.reshape(mod_rows, 1, mod.shape[1])

        w_in_b = w_in[l].astype(bf16)
        g1 = g_norm1[l][None, :]
        hg = jnp.concatenate([jnp.tile(q_norm[l], n_heads) * (HEAD_DIM ** -0.5),
                              jnp.tile(k_norm[l], n_heads)])[None, :]
        proj = _inproj(x2d, mod3, lambda i: (i * tm) // seq, g1, w_in_b, hg, 2 * n_heads, tm)
        hgc = jnp.tile(k_norm[l], n_heads)[None, :]
        kvc = _inproj(ctx2d, mod3, lambda i: batch, g1, w_in_b[:, d_attn:3 * d_attn], hgc,
                      n_heads, tm)

        bias = _attn_bias(rpb[l], seq // GRID_W)
        oattn = _attention(proj, kvc, bias, g_attn_out[l][:, None, :], batch, seq, ctx_len, n_heads)

        x1, h2 = _mix(oattn, proj, x2d, mod3, w_conv[l], g_conv_out[l].reshape(1, d_conv),
                      w_out[l].astype(bf16), g_norm2[l][None, :], seq, tm)

        sk2 = sub_keys[l].reshape(2 * p_heads, sub_keys.shape[3], HEAD_DIM).astype(bf16)
        idx, gate_t = _route(h2, w_pq[l].astype(bf16), sk2, p_heads, tm)

        x2d = _experts(idx, gate_t, h2, x1, mod3, _pack_tables(u_experts[l], v_experts[l]), seq)
    return x2d.reshape(batch, seq, d)
```

```python
import functools

import numpy as np
import jax
import jax.numpy as jnp
from jax import lax
from jax.experimental import pallas as pl
from jax.experimental.pallas import tpu as pltpu

HEAD_DIM = 128
GRID_W = 64
WIN_H = 8
WIN_W = 16
CONV_W = 3
PEER_TOPK = 16
EPS = 1e-6
NEG_INF = -1e30

V7X_VMEM_BYTES = 64 * 1024 * 1024
VMEM_LIMIT_BYTES = V7X_VMEM_BYTES - 8 * 1024 * 1024
LANES = 128

ATTN_Q_ROWS = 4
ATTN_K_ROWS = ATTN_Q_ROWS + WIN_H
ATTN_HEADS = 4

RING = 8
AHEAD = RING - 2
IDX_CHUNK = 4
PEER_STEP = 128
DOT_COPIES = 72

f32 = jnp.float32
bf16 = jnp.bfloat16


def _params(n_grid_dims):
    return pltpu.CompilerParams(
        dimension_semantics=("arbitrary",) * n_grid_dims,
        vmem_limit_bytes=VMEM_LIMIT_BYTES,
    )


def _rms(x):
    return x * lax.rsqrt(jnp.mean(x * x, axis=-1, keepdims=True) + EPS)


def _resident(shape, index_map):
    return pl.BlockSpec(shape, index_map, pipeline_mode=pl.Buffered(1))


def _ada_kernel(c_ref, w_ref, b_ref, o_ref):
    c = c_ref[...]
    cond = (c * jax.nn.sigmoid(c)).astype(bf16)
    o_ref[...] = jnp.dot(cond, w_ref[...].astype(bf16), preferred_element_type=f32) + b_ref[...]


def _ada(cvec, w, b):
    r, d = cvec.shape
    n = w.shape[1]
    tn = 1024
    assert n % tn == 0
    return pl.pallas_call(
        _ada_kernel,
        grid=(n // tn,),
        in_specs=[
            pl.BlockSpec((r, d), lambda j: (0, 0)),
            pl.BlockSpec((d, tn), lambda j: (0, j)),
            pl.BlockSpec((1, tn), lambda j: (0, j)),
        ],
        out_specs=pl.BlockSpec((r, tn), lambda j: (0, j)),
        out_shape=jax.ShapeDtypeStruct((r, n), f32),
        compiler_params=_params(1),
        name="ada_mod",
    )(cvec, w, b)


def _inproj_kernel(x_ref, sh_ref, sc_ref, g_ref, w_ref, hg_ref, o_ref, *, n_norm_heads, cw):
    x = x_ref[...]
    h = (_rms(x) * g_ref[...]) * (1.0 + sc_ref[...]) + sh_ref[...]
    hb = h.astype(bf16)
    n = o_ref.shape[1]
    for j in range(n // cw):
        acc = jnp.dot(hb, w_ref[:, j * cw:(j + 1) * cw], preferred_element_type=f32)
        for k in range(cw // HEAD_DIM):
            head = (j * cw) // HEAD_DIM + k
            blk = acc[:, k * HEAD_DIM:(k + 1) * HEAD_DIM]
            if head < n_norm_heads:
                blk = _rms(blk) * hg_ref[:, head * HEAD_DIM:(head + 1) * HEAD_DIM]
            c0 = head * HEAD_DIM
            o_ref[:, c0:c0 + HEAD_DIM] = blk.astype(bf16)


def _inproj(x2d, mod3, mod_row, g, w, hg, n_norm_heads, tm):
    m, d = x2d.shape
    n = w.shape[1]
    cw = 512
    assert m % tm == 0 and n % cw == 0
    kern = functools.partial(_inproj_kernel, n_norm_heads=n_norm_heads, cw=cw)
    return pl.pallas_call(
        kern,
        grid=(m // tm,),
        in_specs=[
            pl.BlockSpec((tm, d), lambda i: (i, 0)),
            pl.BlockSpec((None, 1, d), lambda i: (mod_row(i), 0, 0)),
            pl.BlockSpec((None, 1, d), lambda i: (mod_row(i), 0, 1)),
            pl.BlockSpec((1, d), lambda i: (0, 0)),
            _resident((d, n), lambda i: (0, 0)),
            pl.BlockSpec((1, hg.shape[1]), lambda i: (0, 0)),
        ],
        out_specs=pl.BlockSpec((tm, n), lambda i: (i, 0)),
        out_shape=jax.ShapeDtypeStruct((m, n), bf16),
        compiler_params=_params(1),
        name="in_proj",
    )(x2d, mod3, mod3, g, w, hg)


def _attn_bias(rpb, rows):
    kh = min(WIN_H, rows)
    n_heads = rpb.shape[0]
    col = np.arange(GRID_W)
    cs = np.clip(col - WIN_W // 2, 0, GRID_W - WIN_W)
    col_ok = (col[None, :] >= cs[:, None]) & (col[None, :] < cs[:, None] + WIN_W)
    dc = np.clip(col[None, :] - col[:, None], -(WIN_W - 1), WIN_W - 1) + (WIN_W - 1)
    onehot = dc[None] == np.arange(2 * WIN_W - 1)[:, None, None]
    by_dr = jnp.sum(jnp.where(onehot[None, None], rpb.astype(f32)[:, :, :, None, None], 0.0), axis=2)
    by_dr = jnp.where(col_ok[None, None], by_dr, NEG_INF)
    masked = jnp.full((n_heads, GRID_W, GRID_W), NEG_INF, f32)
    tiles = []
    for r0 in (0, ATTN_Q_ROWS, rows - ATTN_Q_ROWS):
        ws = int(np.clip(r0 - kh // 2, 0, rows - ATTN_K_ROWS))
        q_rows = []
        for r in range(r0, r0 + ATTN_Q_ROWS):
            rs = int(np.clip(r - kh // 2, 0, rows - kh))
            blocks = [by_dr[:, kr - r + (WIN_H - 1)] if rs <= kr < rs + kh else masked
                      for kr in range(ws, ws + ATTN_K_ROWS)]
            q_rows.append(jnp.concatenate(blocks, axis=2))
        tiles.append(jnp.concatenate(q_rows, axis=1))
    return jnp.stack(tiles, axis=1)


def _attn_kernel(q_ref, k_ref, v_ref, kc_ref, vc_ref, bias_ref, g_ref, o_ref, *, rows):
    grp = pl.program_id(2)
    ws = jnp.clip(ATTN_Q_ROWS * grp - WIN_H // 2, 0, rows - ATTN_K_ROWS)
    start = pl.multiple_of(ws * GRID_W, GRID_W)
    nk = ATTN_K_ROWS * GRID_W
    nt = (((1,), (1,)), ((), ()))
    for hh in range(ATTN_HEADS):
        cols = slice(hh * HEAD_DIM, (hh + 1) * HEAD_DIM)
        q = q_ref[:, cols]
        kw = k_ref[pl.ds(start, nk), cols]
        vw = v_ref[pl.ds(start, nk), cols]
        s_w = lax.dot_general(q, kw, nt, preferred_element_type=f32) + bias_ref[hh]
        s_c = lax.dot_general(q, kc_ref[:, cols], nt, preferred_element_type=f32)
        m = jnp.maximum(jnp.max(s_w, axis=-1, keepdims=True), jnp.max(s_c, axis=-1, keepdims=True))
        p_w = jnp.exp(s_w - m)
        p_c = jnp.exp(s_c - m)
        denom = jnp.sum(p_w, axis=-1, keepdims=True) + jnp.sum(p_c, axis=-1, keepdims=True)
        o = (jnp.dot(p_w.astype(bf16), vw, preferred_element_type=f32)
             + jnp.dot(p_c.astype(bf16), vc_ref[:, cols], preferred_element_type=f32))
        o = o / denom
        o_ref[:, cols] = (_rms(o) * g_ref[hh]).astype(bf16)


def _attention(proj, kvc, bias, g_out, batch, seq, ctx_len, n_heads):
    rows = seq // GRID_W
    n_grp = rows // ATTN_Q_ROWS
    tq = ATTN_Q_ROWS * GRID_W
    nq_blk = seq // tq

    def cfg(g):
        return jnp.where(g == 0, 0, jnp.where(g == n_grp - 1, 2, 1))

    assert n_heads % ATTN_HEADS == 0
    hb = n_heads // ATTN_HEADS
    hw = ATTN_HEADS * HEAD_DIM
    kern = functools.partial(_attn_kernel, rows=rows)
    return pl.pallas_call(
        kern,
        grid=(batch, hb, n_grp),
        in_specs=[
            pl.BlockSpec((tq, hw), lambda b, h, g: (b * nq_blk + g, h)),
            pl.BlockSpec((seq, hw), lambda b, h, g: (b, hb + h)),
            pl.BlockSpec((seq, hw), lambda b, h, g: (b, 2 * hb + h)),
            pl.BlockSpec((ctx_len, hw), lambda b, h, g: (b, h)),
            pl.BlockSpec((ctx_len, hw), lambda b, h, g: (b, hb + h)),
            pl.BlockSpec((ATTN_HEADS, None, tq, ATTN_K_ROWS * GRID_W),
                         lambda b, h, g: (h, cfg(g), 0, 0)),
            pl.BlockSpec((ATTN_HEADS, 1, HEAD_DIM), lambda b, h, g: (h, 0, 0)),
        ],
        out_specs=pl.BlockSpec((tq, hw), lambda b, h, g: (b * nq_blk + g, h)),
        out_shape=jax.ShapeDtypeStruct((batch * seq, n_heads * HEAD_DIM), bf16),
        compiler_params=_params(3),
        name="nbr_attention",
    )(proj, proj, proj, kvc, kvc, bias, g_out)


HALO = 16


def _mix_kernel(oa_ref, bg_ref, cg_ref, u_ref, cgp_ref, up_ref, cgn_ref, un_ref, x_ref, wc_ref,
                gco_ref, wo_ref, gt1_ref, g2_ref, sh2_ref, sc2_ref, x1_ref, h2_ref, *, tm, seq):
    t0 = (pl.program_id(0) * tm) % seq
    z = cg_ref[...].astype(f32) * u_ref[...].astype(f32)
    zp = cgp_ref[HALO - 1:HALO, :].astype(f32) * up_ref[HALO - 1:HALO, :].astype(f32)
    zn = cgn_ref[0:1, :].astype(f32) * un_ref[0:1, :].astype(f32)
    zp = jnp.where(t0 > 0, zp, 0.0)
    zn = jnp.where(t0 + tm < seq, zn, 0.0)
    row = lax.broadcasted_iota(jnp.int32, (tm, 1), 0)
    z_m1 = jnp.where(row == 0, zp, pltpu.roll(z, 1, 0))
    z_p1 = jnp.where(row == tm - 1, zn, pltpu.roll(z, tm - 1, 0))
    y = wc_ref[0:1, :] * z_m1 + wc_ref[1:2, :] * z + wc_ref[2:3, :] * z_p1
    oc = bg_ref[...].astype(f32) * y
    pieces = []
    for gi in range(oc.shape[1] // LANES):
        blk = oc[:, gi * LANES:(gi + 1) * LANES]
        pieces.append((_rms(blk) * gco_ref[:, gi * LANES:(gi + 1) * LANES]).astype(bf16))
    ocn = jnp.concatenate(pieces, axis=1)
    da = oa_ref.shape[1]
    y2 = (jnp.dot(oa_ref[...], wo_ref[0:da, :], preferred_element_type=f32)
          + jnp.dot(ocn, wo_ref[da:, :], preferred_element_type=f32))
    x1 = x_ref[...] + gt1_ref[...] * y2
    x1_ref[...] = x1
    h2 = (_rms(x1) * g2_ref[...]) * (1.0 + sc2_ref[...]) + sh2_ref[...]
    h2_ref[...] = h2.astype(bf16)


def _mix(oattn, proj, x2d, mod3, w_conv, g_conv, w_out, g2, seq, tm):
    m, d = x2d.shape
    da = oattn.shape[1]
    dc = w_conv.shape[1]
    assert dc % LANES == 0 and da % dc == 0 and tm % HALO == 0 and seq % tm == 0
    cb = (3 * da) // dc
    hb = tm // HALO
    n_halo = m // HALO

    def brow(i):
        return (i * tm) // seq

    kern = functools.partial(_mix_kernel, tm=tm, seq=seq)
    return pl.pallas_call(
        kern,
        grid=(m // tm,),
        in_specs=[
            pl.BlockSpec((tm, da), lambda i: (i, 0)),
            pl.BlockSpec((tm, dc), lambda i: (i, cb)),
            pl.BlockSpec((tm, dc), lambda i: (i, cb + 1)),
            pl.BlockSpec((tm, dc), lambda i: (i, cb + 2)),
            pl.BlockSpec((HALO, dc), lambda i: (jnp.maximum(i * hb - 1, 0), cb + 1)),
            pl.BlockSpec((HALO, dc), lambda i: (jnp.maximum(i * hb - 1, 0), cb + 2)),
            pl.BlockSpec((HALO, dc), lambda i: (jnp.minimum((i + 1) * hb, n_halo - 1), cb + 1)),
            pl.BlockSpec((HALO, dc), lambda i: (jnp.minimum((i + 1) * hb, n_halo - 1), cb + 2)),
            pl.BlockSpec((tm, d), lambda i: (i, 0)),
            pl.BlockSpec((CONV_W, dc), lambda i: (0, 0)),
            pl.BlockSpec((1, dc), lambda i: (0, 0)),
            _resident((da + dc, d), lambda i: (0, 0)),
            pl.BlockSpec((None, 1, d), lambda i: (brow(i), 0, 2)),
            pl.BlockSpec((1, d), lambda i: (0, 0)),
            pl.BlockSpec((None, 1, d), lambda i: (brow(i), 0, 3)),
            pl.BlockSpec((None, 1, d), lambda i: (brow(i), 0, 4)),
        ],
        out_specs=[
            pl.BlockSpec((tm, d), lambda i: (i, 0)),
            pl.BlockSpec((tm, d), lambda i: (i, 0)),
        ],
        out_shape=[jax.ShapeDtypeStruct((m, d), f32), jax.ShapeDtypeStruct((m, d), bf16)],
        compiler_params=_params(1),
        name="conv_out_proj",
    )(oattn, proj, proj, proj, proj, proj, proj, proj, x2d, w_conv, g_conv, w_out,
      mod3, g2, mod3, mod3)


def _cand_layout():
    pos = [k2 for k2 in range(PEER_TOPK)]
    for k1 in range(1, 8):
        pos += [k1 * PEER_TOPK + k2 for k2 in range(8)]
    pos += [k1 * PEER_TOPK for k1 in range(8, PEER_TOPK)]
    return np.asarray(pos, np.float32)


N_CAND = 16 + 7 * 8 + 8


def _top16_desc(s, iota_k, iota_t):
    n_keys = s.shape[0]

    def body(t, carry):
        s, vals, idxs = carry
        m = jnp.max(s, axis=0, keepdims=True)
        idx = jnp.min(jnp.where(s == m, iota_k, float(n_keys)), axis=0, keepdims=True)
        s = jnp.where(iota_k == idx, -jnp.inf, s)
        hit = iota_t == t
        return s, jnp.where(hit, m, vals), jnp.where(hit, idx, idxs)

    zeros = jnp.zeros((PEER_TOPK, s.shape[1]), f32)
    _, vals, idxs = lax.fori_loop(0, PEER_TOPK, body, (s, zeros, zeros))
    return vals, idxs


def _route_kernel(h2_ref, wq_ref, sk_ref, cpos_ref, idx_ref, gate_ref, s_scr, e_scr, g_scr,
                  *, n_heads, tm):
    n_keys = sk_ref.shape[1]
    nt = (((1,), (1,)), ((), ()))
    qf = jnp.dot(h2_ref[...], wq_ref[...], preferred_element_type=f32).astype(bf16)
    for hp in range(2 * n_heads):
        s_scr[hp] = lax.dot_general(sk_ref[hp], qf[:, hp * HEAD_DIM:(hp + 1) * HEAD_DIM], nt,
                                    preferred_element_type=f32)

    iota_k = lax.broadcasted_iota(jnp.int32, (n_keys, 2 * LANES), 0).astype(f32)
    iota_t2 = lax.broadcasted_iota(jnp.int32, (PEER_TOPK, 2 * LANES), 0)
    iota_t = lax.broadcasted_iota(jnp.int32, (PEER_TOPK, tm), 0)
    cpos = cpos_ref[...]
    n_lb = tm // LANES

    def head_body(h, _):
        cands, i1s, i2s = [], [], []
        for lb in range(n_lb):
            s1 = s_scr[2 * h, :, lb * LANES:(lb + 1) * LANES]
            s2 = s_scr[2 * h + 1, :, lb * LANES:(lb + 1) * LANES]
            a12, i12 = _top16_desc(jnp.concatenate([s1, s2], axis=1), iota_k, iota_t2)
            a1, a2 = a12[:, :LANES], a12[:, LANES:]
            i1s.append(i12[:, :LANES])
            i2s.append(i12[:, LANES:])
            pieces = [a1[0:1, :] + a2]
            for k1 in range(1, 8):
                pieces.append(a1[k1:k1 + 1, :] + a2[0:8, :])
            pieces.append(a1[8:16, :] + a2[0:1, :])
            cands.append(jnp.concatenate(pieces, axis=0))
        cand = jnp.concatenate(cands, axis=1)
        i1 = jnp.concatenate(i1s, axis=1)
        i2 = jnp.concatenate(i2s, axis=1)

        def pick(t, carry):
            cand, best, bpos = carry
            m = jnp.max(cand, axis=0, keepdims=True)
            p = jnp.min(jnp.where(cand == m, cpos, 1e9), axis=0, keepdims=True)
            cand = jnp.where(cpos == p, -jnp.inf, cand)
            hit = iota_t == t
            return cand, jnp.where(hit, m, best), jnp.where(hit, p, bpos)

        zeros = jnp.zeros((PEER_TOPK, tm), f32)
        _, best, bpos = lax.fori_loop(0, PEER_TOPK, pick, (cand, zeros, zeros))
        k1f = jnp.floor(bpos * (1.0 / PEER_TOPK))
        k2f = bpos - PEER_TOPK * k1f
        e1 = jnp.zeros_like(bpos)
        e2 = jnp.zeros_like(bpos)
        for j in range(PEER_TOPK):
            e1 = e1 + jnp.where(k1f == float(j), i1[j:j + 1, :], 0.0)
            e2 = e2 + jnp.where(k2f == float(j), i2[j:j + 1, :], 0.0)
        expert = e1 * float(n_keys) + e2
        ex = jnp.exp(best - best[0:1, :])
        gate = ex / jnp.sum(ex, axis=0, keepdims=True)
        r0 = pl.multiple_of(h * PEER_TOPK, PEER_TOPK)
        e_scr[pl.ds(r0, PEER_TOPK), :] = expert
        g_scr[pl.ds(r0, PEER_TOPK), :] = gate
        return 0

    lax.fori_loop(0, n_heads, head_body, 0)
    for lb in range(n_lb):
        idx_ref[lb * LANES:(lb + 1) * LANES, :] = (
            e_scr[:, lb * LANES:(lb + 1) * LANES].T.astype(jnp.int32))
    gate_ref[...] = g_scr[...]


def _route(h2, w_pq, sub_keys2, n_heads, tm):
    m, d = h2.shape
    nq = w_pq.shape[1]
    n_keys = sub_keys2.shape[1]
    ne = n_heads * PEER_TOPK
    assert ne == LANES and tm % LANES == 0 and m % tm == 0
    cpos = jnp.asarray(np.broadcast_to(_cand_layout()[:, None], (N_CAND, tm)).copy())
    kern = functools.partial(_route_kernel, n_heads=n_heads, tm=tm)
    return pl.pallas_call(
        kern,
        grid=(m // tm,),
        in_specs=[
            pl.BlockSpec((tm, d), lambda i: (i, 0)),
            _resident((d, nq), lambda i: (0, 0)),
            _resident((2 * n_heads, n_keys, HEAD_DIM), lambda i: (0, 0, 0)),
            pl.BlockSpec((N_CAND, tm), lambda i: (0, 0)),
        ],
        out_specs=[
            pl.BlockSpec((tm, ne), lambda i: (i, 0)),
            pl.BlockSpec((ne, tm), lambda i: (0, i)),
        ],
        out_shape=[jax.ShapeDtypeStruct((m, ne), jnp.int32), jax.ShapeDtypeStruct((ne, m), f32)],
        scratch_shapes=[
            pltpu.VMEM((2 * n_heads, n_keys, tm), f32),
            pltpu.VMEM((ne, tm), f32),
            pltpu.VMEM((ne, tm), f32),
        ],
        compiler_params=_params(1),
        name="peer_route",
    )(h2, w_pq, sub_keys2, cpos)


SLAB = 16


def _pack_kernel(u_ref, v_ref, o_ref, scr):
    te = u_ref.shape[0]
    n_tiles = u_ref.shape[1] // LANES
    for half, src in enumerate((u_ref, v_ref)):
        for rg in range(te // 8):
            for j in range(n_tiles):
                scr[pl.ds(rg * 8 * n_tiles + j, 8, stride=n_tiles), :] = (
                    src[rg * 8:(rg + 1) * 8, j * LANES:(j + 1) * LANES])
        o_ref[:, half * n_tiles:(half + 1) * n_tiles, :] = (
            scr[...].reshape(te, n_tiles, LANES).astype(bf16))


def _pack_tables(u, v):
    n_exp, d = u.shape
    te = 256
    assert n_exp % te == 0 and d == SLAB * LANES
    return pl.pallas_call(
        _pack_kernel,
        grid=(n_exp // te,),
        in_specs=[pl.BlockSpec((te, d), lambda i: (i, 0)), pl.BlockSpec((te, d), lambda i: (i, 0))],
        out_specs=pl.BlockSpec((te, 2 * SLAB, LANES), lambda i: (i, 0, 0)),
        out_shape=jax.ShapeDtypeStruct((n_exp, 2 * SLAB, LANES), bf16),
        scratch_shapes=[pltpu.VMEM((te * SLAB, LANES), f32)],
        compiler_params=_params(1),
        name="pack_tables",
    )(u, v)


def _sublane_sums(r, sub):
    def merge(a, b, half):
        lo = (sub % (2 * half)) < half
        if 2 * half == 8:
            return jnp.where(lo, a, b) + pltpu.roll(jnp.where(lo, b, a), half, 0)
        x = jnp.where(lo, a, pltpu.roll(b, half, 0))
        y = jnp.where(lo, pltpu.roll(a, 8 - half, 0), b)
        return x + y
    a = [merge(r[j], r[j + 4], 4) for j in range(4)]
    b = [merge(a[j], a[j + 2], 2) for j in range(2)]
    return merge(b[0], b[1], 1)


def _expert_kernel(idx_hbm, uv_hbm, gate_ref, h2_ref, x1_ref, gt2_ref, o_ref, idx_smem, *rest, d, ne):
    rings, acts = rest[:RING], rest[RING:2 * RING]
    hs_scr, ys_scr, idx_sem, row_sem = rest[2 * RING:]
    step = pl.program_id(0)
    n_steps = pl.num_programs(0)
    n_tiles = d // LANES
    n_groups = ne // 8
    last_chunk = n_steps * (PEER_STEP // IDX_CHUNK) - 1
    lane = lax.broadcasted_iota(jnp.int32, (ne, PEER_STEP), 1)
    sub = lax.broadcasted_iota(jnp.int32, (8, LANES), 0)

    def idx_copy(chunk, buf):
        c = jnp.minimum(chunk, last_chunk)
        return pltpu.make_async_copy(idx_hbm.at[pl.ds(c * IDX_CHUNK, IDX_CHUNK)], idx_smem.at[buf],
                                     idx_sem.at[buf])

    def row_copy(t, e):
        buf, r = (t // IDX_CHUNK) % 2, t % IDX_CHUNK
        return pltpu.make_async_copy(uv_hbm.at[idx_smem[buf, r, e]], rings[t % RING].at[e],
                                     row_sem.at[t % RING])

    def wait_rows(slot):
        pltpu.make_async_copy(uv_hbm.at[pl.ds(0, ne)], rings[slot], row_sem.at[slot]).wait()

    @pl.when(step == 0)
    def _():
        idx_copy(0, 0).start()
        idx_copy(1, 1).start()
        idx_copy(0, 0).wait()
        idx_copy(1, 1).wait()

        def prime(e, _):
            for t in range(AHEAD):
                row_copy(t, e).start()
            return 0
        lax.fori_loop(0, ne, prime, 0)
        idx_copy(2, 0).start()

    hf = h2_ref[...].astype(f32)
    for rg in range(PEER_STEP // 8):
        for j in range(n_tiles):
            hs_scr[pl.ds(rg * 8 * n_tiles + j, 8, stride=n_tiles), :] = (
                hf[rg * 8:(rg + 1) * 8, j * LANES:(j + 1) * LANES])

    def ring_body(it, _):
        t0 = it * RING
        first_chunk = (step * PEER_STEP + t0) // IDX_CHUNK

        def start_copies(k, lo, hi):
            for e in range(lo, hi):
                row_copy(k + AHEAD, e).start(priority=e % 2)

        def open_token(k):
            if (k + AHEAD) % IDX_CHUNK == 0:
                q = (k + AHEAD) // IDX_CHUNK
                idx_copy(first_chunk + q, q % 2).wait()
                idx_copy(first_chunk + q + 1, 1 - q % 2).start()
            wait_rows(k)
            gcol = jnp.sum(jnp.where(lane == t0 + k, gate_ref[...], 0.0), axis=1, keepdims=True)
            hs = hs_scr[pl.ds(pl.multiple_of((t0 + k) * n_tiles, n_tiles), n_tiles), :]
            return gcol, hs

        def dot(k, hs, groups, n_copies):
            cols = []
            for g in groups:
                parts = []
                for j in range(8):
                    p = rings[k][g * 8 + j, 0:SLAB, :].astype(f32) * hs
                    parts.append(p[0:8, :] + p[8:16, :])
                cols.append(jnp.sum(_sublane_sums(parts, sub), axis=1, keepdims=True))
                start_copies(k, g * n_copies // n_groups, (g + 1) * n_copies // n_groups)
            return cols

        def activate(k, cols, gcol):
            s = jnp.concatenate(cols, axis=0)
            act = 0.5 * s * (1.0 + lax.erf(s * (2.0 ** -0.5))) * gcol
            acts[k][...] = jnp.broadcast_to(act, (ne, LANES))

        def axpy(k, stage, first_copy):
            accs = [jnp.zeros((SLAB, LANES), f32) for _ in range(4)]
            n_copies = ne - first_copy
            for g in range(n_groups):
                for e in range(g * 8, g * 8 + 8):
                    v = rings[k][e, SLAB:2 * SLAB, :].astype(f32)
                    accs[e % 4] = accs[e % 4] + acts[k][e:e + 1, :] * v
                start_copies(stage, first_copy + g * n_copies // n_groups,
                             first_copy + (g + 1) * n_copies // n_groups)
            ys_scr[pl.ds(pl.multiple_of((t0 + k) * n_tiles, n_tiles), n_tiles), :] = (
                (accs[0] + accs[1]) + (accs[2] + accs[3]))

        half = n_groups // 2
        gcol, hs = open_token(0)
        cols = dot(0, hs, range(n_groups), ne)
        for k in range(1, RING):
            gcol_k, hs = open_token(k)
            cols_k = dot(k, hs, range(half), DOT_COPIES)
            activate(k - 1, cols, gcol)
            cols_k += dot(k, hs, range(half, n_groups), DOT_COPIES)
            axpy(k - 1, k, DOT_COPIES)
            cols, gcol = cols_k, gcol_k
        activate(RING - 1, cols, gcol)
        axpy(RING - 1, RING - 1, ne)
        return 0

    lax.fori_loop(0, PEER_STEP // RING, ring_body, 0)

    for rg in range(PEER_STEP // 8):
        y = jnp.concatenate([ys_scr[pl.ds(rg * 8 * n_tiles + j, 8, stride=n_tiles), :]
                             for j in range(n_tiles)], axis=1)
        o_ref[rg * 8:(rg + 1) * 8, :] = x1_ref[rg * 8:(rg + 1) * 8, :] + gt2_ref[...] * y

    @pl.when(step == n_steps - 1)
    def _():
        for t in range(AHEAD):
            wait_rows(t % RING)
        last_fetch_stage = max(k for k in range(RING) if (k + AHEAD) % IDX_CHUNK == 0)
        idx_copy(last_chunk, 1 - ((last_fetch_stage + AHEAD) // IDX_CHUNK) % 2).wait()


def _experts(idx, gate_t, h2, x1, mod3, uv, seq):
    m, d = h2.shape
    ne = idx.shape[1]
    assert d == SLAB * LANES and uv.shape[1:] == (2 * SLAB, LANES) and ne % 8 == 0
    assert m % PEER_STEP == 0 and seq % PEER_STEP == 0
    assert RING % (2 * IDX_CHUNK) == 0 and PEER_STEP % RING == 0 and AHEAD <= 2 * IDX_CHUNK
    kern = functools.partial(_expert_kernel, d=d, ne=ne)
    return pl.pallas_call(
        kern,
        grid=(m // PEER_STEP,),
        in_specs=[
            pl.BlockSpec(memory_space=pl.ANY),
            pl.BlockSpec(memory_space=pl.ANY),
            pl.BlockSpec((ne, PEER_STEP), lambda i: (0, i)),
            pl.BlockSpec((PEER_STEP, d), lambda i: (i, 0)),
            pl.BlockSpec((PEER_STEP, d), lambda i: (i, 0)),
            pl.BlockSpec((None, 1, d), lambda i: ((i * PEER_STEP) // seq, 0, 5)),
        ],
        out_specs=pl.BlockSpec((PEER_STEP, d), lambda i: (i, 0)),
        out_shape=jax.ShapeDtypeStruct((m, d), f32),
        scratch_shapes=(
            [pltpu.SMEM((2, IDX_CHUNK, ne), jnp.int32)]
            + [pltpu.VMEM((ne, 2 * SLAB, LANES), bf16) for _ in range(RING)]
            + [pltpu.VMEM((ne, LANES), f32) for _ in range(RING)]
            + [pltpu.VMEM((PEER_STEP * SLAB, LANES), f32),
               pltpu.VMEM((PEER_STEP * SLAB, LANES), f32),
               pltpu.SemaphoreType.DMA((2,)),
               pltpu.SemaphoreType.DMA((RING,))]),
        compiler_params=_params(1),
        name="peer_experts",
    )(idx, uv, gate_t, h2, x1, mod3)


def kernel(x, c, ctx, c_ctx, w_ada, b_ada, g_norm1, w_in, q_norm, k_norm, rpb, w_conv,
           g_attn_out, g_conv_out, w_out, g_norm2, w_pq, sub_keys, u_experts, v_experts):
    batch, seq, d = x.shape
    ctx_len = ctx.shape[1]
    depth = w_ada.shape[0]
    n_heads = rpb.shape[1]
    d_attn = n_heads * HEAD_DIM
    d_conv = w_conv.shape[2]
    p_heads = sub_keys.shape[1]
    assert d % LANES == 0 and seq % (ATTN_Q_ROWS * GRID_W) == 0 and seq // GRID_W >= ATTN_K_ROWS
    assert w_in.shape[2] == 3 * d_attn + 3 * d_conv and g_conv_out.shape[2] == LANES
    assert sub_keys.shape[2] == 2 and sub_keys.shape[4] == HEAD_DIM
    tm = 256

    x2d = x.reshape(batch * seq, d)
    ctx2d = ctx.reshape(batch * ctx_len, d)
    mod_rows = -(-(batch + 1) // 8) * 8
    cvec = jnp.zeros((mod_rows, d), f32).at[:batch].set(c).at[batch].set(c_ctx)

    for l in range(depth):
        assert l == depth - 1
        mod = _ada(cvec, w_ada[l], b_ada[l][None, :])
        mod3 = mod.reshape(mod_rows, 1, mod.shape[1])

        w_in_b = w_in[l].astype(bf16)
        g1 = g_norm1[l][None, :]
        hg = jnp.concatenate([jnp.tile(q_norm[l], n_heads) * (HEAD_DIM ** -0.5),
                              jnp.tile(k_norm[l], n_heads)])[None, :]
        proj = _inproj(x2d, mod3, lambda i: (i * tm) // seq, g1, w_in_b, hg, 2 * n_heads, tm)
        hgc = jnp.tile(k_norm[l], n_heads)[None, :]
        kvc = _inproj(ctx2d, mod3, lambda i: batch, g1, w_in_b[:, d_attn:3 * d_attn], hgc,
                      n_heads, tm)

        bias = _attn_bias(rpb[l], seq // GRID_W)
        oattn = _attention(proj, kvc, bias, g_attn_out[l][:, None, :], batch, seq, ctx_len, n_heads)

        x1, h2 = _mix(oattn, proj, x2d, mod3, w_conv[l], g_conv_out[l].reshape(1, d_conv),
                      w_out[l].astype(bf16), g_norm2[l][None, :], seq, tm)

        sk2 = sub_keys[l].reshape(2 * p_heads, sub_keys.shape[3], HEAD_DIM).astype(bf16)
        idx, gate_t = _route(h2, w_pq[l].astype(bf16), sk2, p_heads, tm)

        x2d = _experts(idx, gate_t, h2, x1, mod3, _pack_tables(u_experts[l], v_experts[l]), seq)
    return x2d.reshape(batch, seq, d)
```

```python
import functools

import numpy as np
import jax
import jax.numpy as jnp
from jax import lax
from jax.experimental import pallas as pl
from jax.experimental.pallas import tpu as pltpu

HEAD_DIM = 128
GRID_W = 64
WIN_H = 8
WIN_W = 16
CONV_W = 3
PEER_TOPK = 16
EPS = 1e-6
NEG_INF = -1e30

V7X_VMEM_BYTES = 64 * 1024 * 1024
VMEM_LIMIT_BYTES = V7X_VMEM_BYTES - 8 * 1024 * 1024
LANES = 128

ATTN_Q_ROWS = 4
ATTN_K_ROWS = ATTN_Q_ROWS + WIN_H
ATTN_HEADS = 4

RING = 8
AHEAD = RING - 2
IDX_CHUNK = 4
PEER_STEP = 128
DOT_COPIES = 72

f32 = jnp.float32
bf16 = jnp.bfloat16


def _params(n_grid_dims, fuse_operand=None, n_operands=0):
    fusion = None
    if fuse_operand is not None:
        fusion = [i == fuse_operand for i in range(n_operands)]
    return pltpu.CompilerParams(
        dimension_semantics=("arbitrary",) * n_grid_dims,
        vmem_limit_bytes=VMEM_LIMIT_BYTES,
        allow_input_fusion=fusion,
    )


def _rms(x):
    return x * lax.rsqrt(jnp.mean(x * x, axis=-1, keepdims=True) + EPS)


def _resident(shape, index_map):
    return pl.BlockSpec(shape, index_map, pipeline_mode=pl.Buffered(1))


def _ada_kernel(c_ref, w_ref, b_ref, o_ref):
    c = c_ref[...]
    cond = (c * jax.nn.sigmoid(c)).astype(bf16)
    o_ref[...] = jnp.dot(cond, w_ref[...].astype(bf16), preferred_element_type=f32) + b_ref[...]


def _ada(cvec, w, b):
    r, d = cvec.shape
    n = w.shape[1]
    tn = 1024
    assert n % tn == 0
    return pl.pallas_call(
        _ada_kernel,
        grid=(n // tn,),
        in_specs=[
            pl.BlockSpec((r, d), lambda j: (0, 0)),
            pl.BlockSpec((d, tn), lambda j: (0, j)),
            pl.BlockSpec((1, tn), lambda j: (0, j)),
        ],
        out_specs=pl.BlockSpec((r, tn), lambda j: (0, j)),
        out_shape=jax.ShapeDtypeStruct((r, n), f32),
        compiler_params=_params(1),
        name="ada_mod",
    )(cvec, w, b)


def _inproj_kernel(x_ref, sh_ref, sc_ref, g_ref, w_ref, hg_ref, o_ref, *, n_norm_heads, cw):
    x = x_ref[...]
    h = (_rms(x) * g_ref[...]) * (1.0 + sc_ref[...]) + sh_ref[...]
    hb = h.astype(bf16)
    n = o_ref.shape[1]
    for j in range(n // cw):
        acc = jnp.dot(hb, w_ref[:, j * cw:(j + 1) * cw], preferred_element_type=f32)
        for k in range(cw // HEAD_DIM):
            head = (j * cw) // HEAD_DIM + k
            blk = acc[:, k * HEAD_DIM:(k + 1) * HEAD_DIM]
            if head < n_norm_heads:
                blk = _rms(blk) * hg_ref[:, head * HEAD_DIM:(head + 1) * HEAD_DIM]
            c0 = head * HEAD_DIM
            o_ref[:, c0:c0 + HEAD_DIM] = blk.astype(bf16)


def _inproj(x2d, mod3, mod_row, g, w, hg, n_norm_heads, tm):
    m, d = x2d.shape
    n = w.shape[1]
    cw = 512
    assert m % tm == 0 and n % cw == 0
    kern = functools.partial(_inproj_kernel, n_norm_heads=n_norm_heads, cw=cw)
    return pl.pallas_call(
        kern,
        grid=(m // tm,),
        in_specs=[
            pl.BlockSpec((tm, d), lambda i: (i, 0)),
            pl.BlockSpec((None, 1, d), lambda i: (mod_row(i), 0, 0)),
            pl.BlockSpec((None, 1, d), lambda i: (mod_row(i), 0, 1)),
            pl.BlockSpec((1, d), lambda i: (0, 0)),
            _resident((d, n), lambda i: (0, 0)),
            pl.BlockSpec((1, hg.shape[1]), lambda i: (0, 0)),
        ],
        out_specs=pl.BlockSpec((tm, n), lambda i: (i, 0)),
        out_shape=jax.ShapeDtypeStruct((m, n), bf16),
        compiler_params=_params(1, fuse_operand=4, n_operands=6),
        name="in_proj",
    )(x2d, mod3, mod3, g, w, hg)


def _attn_bias(rpb, rows):
    kh = min(WIN_H, rows)
    n_heads = rpb.shape[0]
    col = np.arange(GRID_W)
    cs = np.clip(col - WIN_W // 2, 0, GRID_W - WIN_W)
    col_ok = (col[None, :] >= cs[:, None]) & (col[None, :] < cs[:, None] + WIN_W)
    dc = np.clip(col[None, :] - col[:, None], -(WIN_W - 1), WIN_W - 1) + (WIN_W - 1)
    onehot = dc[None] == np.arange(2 * WIN_W - 1)[:, None, None]
    by_dr = jnp.sum(jnp.where(onehot[None, None], rpb.astype(f32)[:, :, :, None, None], 0.0), axis=2)
    by_dr = jnp.where(col_ok[None, None], by_dr, NEG_INF)
    masked = jnp.full((n_heads, GRID_W, GRID_W), NEG_INF, f32)
    tiles = []
    for r0 in (0, ATTN_Q_ROWS, rows - ATTN_Q_ROWS):
        ws = int(np.clip(r0 - kh // 2, 0, rows - ATTN_K_ROWS))
        q_rows = []
        for r in range(r0, r0 + ATTN_Q_ROWS):
            rs = int(np.clip(r - kh // 2, 0, rows - kh))
            blocks = [by_dr[:, kr - r + (WIN_H - 1)] if rs <= kr < rs + kh else masked
                      for kr in range(ws, ws + ATTN_K_ROWS)]
            q_rows.append(jnp.concatenate(blocks, axis=2))
        tiles.append(jnp.concatenate(q_rows, axis=1))
    return jnp.stack(tiles, axis=1)


def _attn_kernel(q_ref, k_ref, v_ref, kc_ref, vc_ref, bias_ref, g_ref, o_ref, *, rows):
    grp = pl.program_id(2)
    ws = jnp.clip(ATTN_Q_ROWS * grp - WIN_H // 2, 0, rows - ATTN_K_ROWS)
    start = pl.multiple_of(ws * GRID_W, GRID_W)
    nk = ATTN_K_ROWS * GRID_W
    nt = (((1,), (1,)), ((), ()))
    for hh in range(ATTN_HEADS):
        cols = slice(hh * HEAD_DIM, (hh + 1) * HEAD_DIM)
        q = q_ref[:, cols]
        kw = k_ref[pl.ds(start, nk), cols]
        vw = v_ref[pl.ds(start, nk), cols]
        s_w = lax.dot_general(q, kw, nt, preferred_element_type=f32) + bias_ref[hh]
        s_c = lax.dot_general(q, kc_ref[:, cols], nt, preferred_element_type=f32)
        m = jnp.maximum(jnp.max(s_w, axis=-1, keepdims=True), jnp.max(s_c, axis=-1, keepdims=True))
        p_w = jnp.exp(s_w - m)
        p_c = jnp.exp(s_c - m)
        denom = jnp.sum(p_w, axis=-1, keepdims=True) + jnp.sum(p_c, axis=-1, keepdims=True)
        o = (jnp.dot(p_w.astype(bf16), vw, preferred_element_type=f32)
             + jnp.dot(p_c.astype(bf16), vc_ref[:, cols], preferred_element_type=f32))
        o = o / denom
        o_ref[:, cols] = (_rms(o) * g_ref[hh]).astype(bf16)


def _attention(proj, kvc, bias, g_out, batch, seq, ctx_len, n_heads):
    rows = seq // GRID_W
    n_grp = rows // ATTN_Q_ROWS
    tq = ATTN_Q_ROWS * GRID_W
    nq_blk = seq // tq

    def cfg(g):
        return jnp.where(g == 0, 0, jnp.where(g == n_grp - 1, 2, 1))

    assert n_heads % ATTN_HEADS == 0
    hb = n_heads // ATTN_HEADS
    hw = ATTN_HEADS * HEAD_DIM
    kern = functools.partial(_attn_kernel, rows=rows)
    return pl.pallas_call(
        kern,
        grid=(batch, hb, n_grp),
        in_specs=[
            pl.BlockSpec((tq, hw), lambda b, h, g: (b * nq_blk + g, h)),
            pl.BlockSpec((seq, hw), lambda b, h, g: (b, hb + h)),
            pl.BlockSpec((seq, hw), lambda b, h, g: (b, 2 * hb + h)),
            pl.BlockSpec((ctx_len, hw), lambda b, h, g: (b, h)),
            pl.BlockSpec((ctx_len, hw), lambda b, h, g: (b, hb + h)),
            pl.BlockSpec((ATTN_HEADS, None, tq, ATTN_K_ROWS * GRID_W),
                         lambda b, h, g: (h, cfg(g), 0, 0)),
            pl.BlockSpec((ATTN_HEADS, 1, HEAD_DIM), lambda b, h, g: (h, 0, 0)),
        ],
        out_specs=pl.BlockSpec((tq, hw), lambda b, h, g: (b * nq_blk + g, h)),
        out_shape=jax.ShapeDtypeStruct((batch * seq, n_heads * HEAD_DIM), bf16),
        compiler_params=_params(3),
        name="nbr_attention",
    )(proj, proj, proj, kvc, kvc, bias, g_out)


HALO = 16


def _mix_kernel(oa_ref, bg_ref, cg_ref, u_ref, cgp_ref, up_ref, cgn_ref, un_ref, x_ref, wc_ref,
                gco_ref, wo_ref, gt1_ref, g2_ref, sh2_ref, sc2_ref, x1_ref, h2_ref, *, tm, seq):
    t0 = (pl.program_id(0) * tm) % seq
    z = cg_ref[...].astype(f32) * u_ref[...].astype(f32)
    zp = cgp_ref[HALO - 1:HALO, :].astype(f32) * up_ref[HALO - 1:HALO, :].astype(f32)
    zn = cgn_ref[0:1, :].astype(f32) * un_ref[0:1, :].astype(f32)
    zp = jnp.where(t0 > 0, zp, 0.0)
    zn = jnp.where(t0 + tm < seq, zn, 0.0)
    row = lax.broadcasted_iota(jnp.int32, (tm, 1), 0)
    z_m1 = jnp.where(row == 0, zp, pltpu.roll(z, 1, 0))
    z_p1 = jnp.where(row == tm - 1, zn, pltpu.roll(z, tm - 1, 0))
    y = wc_ref[0:1, :] * z_m1 + wc_ref[1:2, :] * z + wc_ref[2:3, :] * z_p1
    oc = bg_ref[...].astype(f32) * y
    pieces = []
    for gi in range(oc.shape[1] // LANES):
        blk = oc[:, gi * LANES:(gi + 1) * LANES]
        pieces.append((_rms(blk) * gco_ref[:, gi * LANES:(gi + 1) * LANES]).astype(bf16))
    ocn = jnp.concatenate(pieces, axis=1)
    da = oa_ref.shape[1]
    y2 = (jnp.dot(oa_ref[...], wo_ref[0:da, :], preferred_element_type=f32)
          + jnp.dot(ocn, wo_ref[da:, :], preferred_element_type=f32))
    x1 = x_ref[...] + gt1_ref[...] * y2
    x1_ref[...] = x1
    h2 = (_rms(x1) * g2_ref[...]) * (1.0 + sc2_ref[...]) + sh2_ref[...]
    h2_ref[...] = h2.astype(bf16)


def _mix(oattn, proj, x2d, mod3, w_conv, g_conv, w_out, g2, seq, tm):
    m, d = x2d.shape
    da = oattn.shape[1]
    dc = w_conv.shape[1]
    assert dc % LANES == 0 and da % dc == 0 and tm % HALO == 0 and seq % tm == 0
    cb = (3 * da) // dc
    hb = tm // HALO
    n_halo = m // HALO

    def brow(i):
        return (i * tm) // seq

    kern = functools.partial(_mix_kernel, tm=tm, seq=seq)
    return pl.pallas_call(
        kern,
        grid=(m // tm,),
        in_specs=[
            pl.BlockSpec((tm, da), lambda i: (i, 0)),
            pl.BlockSpec((tm, dc), lambda i: (i, cb)),
            pl.BlockSpec((tm, dc), lambda i: (i, cb + 1)),
            pl.BlockSpec((tm, dc), lambda i: (i, cb + 2)),
            pl.BlockSpec((HALO, dc), lambda i: (jnp.maximum(i * hb - 1, 0), cb + 1)),
            pl.BlockSpec((HALO, dc), lambda i: (jnp.maximum(i * hb - 1, 0), cb + 2)),
            pl.BlockSpec((HALO, dc), lambda i: (jnp.minimum((i + 1) * hb, n_halo - 1), cb + 1)),
            pl.BlockSpec((HALO, dc), lambda i: (jnp.minimum((i + 1) * hb, n_halo - 1), cb + 2)),
            pl.BlockSpec((tm, d), lambda i: (i, 0)),
            pl.BlockSpec((CONV_W, dc), lambda i: (0, 0)),
            pl.BlockSpec((1, dc), lambda i: (0, 0)),
            _resident((da + dc, d), lambda i: (0, 0)),
            pl.BlockSpec((None, 1, d), lambda i: (brow(i), 0, 2)),
            pl.BlockSpec((1, d), lambda i: (0, 0)),
            pl.BlockSpec((None, 1, d), lambda i: (brow(i), 0, 3)),
            pl.BlockSpec((None, 1, d), lambda i: (brow(i), 0, 4)),
        ],
        out_specs=[
            pl.BlockSpec((tm, d), lambda i: (i, 0)),
            pl.BlockSpec((tm, d), lambda i: (i, 0)),
        ],
        out_shape=[jax.ShapeDtypeStruct((m, d), f32), jax.ShapeDtypeStruct((m, d), bf16)],
        compiler_params=_params(1, fuse_operand=11, n_operands=16),
        name="conv_out_proj",
    )(oattn, proj, proj, proj, proj, proj, proj, proj, x2d, w_conv, g_conv, w_out,
      mod3, g2, mod3, mod3)


def _cand_layout():
    pos = [k2 for k2 in range(PEER_TOPK)]
    for k1 in range(1, 8):
        pos += [k1 * PEER_TOPK + k2 for k2 in range(8)]
    pos += [k1 * PEER_TOPK for k1 in range(8, PEER_TOPK)]
    return np.asarray(pos, np.float32)


N_CAND = 16 + 7 * 8 + 8


def _top16_desc(s, iota_k, iota_t):
    n_keys = s.shape[0]

    def body(t, carry):
        s, vals, idxs = carry
        m = jnp.max(s, axis=0, keepdims=True)
        idx = jnp.min(jnp.where(s == m, iota_k, float(n_keys)), axis=0, keepdims=True)
        s = jnp.where(iota_k == idx, -jnp.inf, s)
        hit = iota_t == t
        return s, jnp.where(hit, m, vals), jnp.where(hit, idx, idxs)

    zeros = jnp.zeros((PEER_TOPK, s.shape[1]), f32)
    _, vals, idxs = lax.fori_loop(0, PEER_TOPK, body, (s, zeros, zeros))
    return vals, idxs


def _route_kernel(h2_ref, wq_ref, sk_ref, cpos_ref, idx_ref, gate_ref, s_scr, e_scr, g_scr,
                  *, n_heads, tm):
    n_keys = sk_ref.shape[1]
    nt = (((1,), (1,)), ((), ()))
    qf = jnp.dot(h2_ref[...], wq_ref[...], preferred_element_type=f32).astype(bf16)
    for hp in range(2 * n_heads):
        s_scr[hp] = lax.dot_general(sk_ref[hp], qf[:, hp * HEAD_DIM:(hp + 1) * HEAD_DIM], nt,
                                    preferred_element_type=f32)

    iota_k = lax.broadcasted_iota(jnp.int32, (n_keys, 2 * LANES), 0).astype(f32)
    iota_t2 = lax.broadcasted_iota(jnp.int32, (PEER_TOPK, 2 * LANES), 0)
    iota_t = lax.broadcasted_iota(jnp.int32, (PEER_TOPK, tm), 0)
    cpos = cpos_ref[...]
    n_lb = tm // LANES

    def head_body(h, _):
        cands, i1s, i2s = [], [], []
        for lb in range(n_lb):
            s1 = s_scr[2 * h, :, lb * LANES:(lb + 1) * LANES]
            s2 = s_scr[2 * h + 1, :, lb * LANES:(lb + 1) * LANES]
            a12, i12 = _top16_desc(jnp.concatenate([s1, s2], axis=1), iota_k, iota_t2)
            a1, a2 = a12[:, :LANES], a12[:, LANES:]
            i1s.append(i12[:, :LANES])
            i2s.append(i12[:, LANES:])
            pieces = [a1[0:1, :] + a2]
            for k1 in range(1, 8):
                pieces.append(a1[k1:k1 + 1, :] + a2[0:8, :])
            pieces.append(a1[8:16, :] + a2[0:1, :])
            cands.append(jnp.concatenate(pieces, axis=0))
        cand = jnp.concatenate(cands, axis=1)
        i1 = jnp.concatenate(i1s, axis=1)
        i2 = jnp.concatenate(i2s, axis=1)

        def pick(t, carry):
            cand, best, bpos = carry
            m = jnp.max(cand, axis=0, keepdims=True)
            p = jnp.min(jnp.where(cand == m, cpos, 1e9), axis=0, keepdims=True)
            cand = jnp.where(cpos == p, -jnp.inf, cand)
            hit = iota_t == t
            return cand, jnp.where(hit, m, best), jnp.where(hit, p, bpos)

        zeros = jnp.zeros((PEER_TOPK, tm), f32)
        _, best, bpos = lax.fori_loop(0, PEER_TOPK, pick, (cand, zeros, zeros))
        k1f = jnp.floor(bpos * (1.0 / PEER_TOPK))
        k2f = bpos - PEER_TOPK * k1f
        e1 = jnp.zeros_like(bpos)
        e2 = jnp.zeros_like(bpos)
        for j in range(PEER_TOPK):
            e1 = e1 + jnp.where(k1f == float(j), i1[j:j + 1, :], 0.0)
            e2 = e2 + jnp.where(k2f == float(j), i2[j:j + 1, :], 0.0)
        expert = e1 * float(n_keys) + e2
        ex = jnp.exp(best - best[0:1, :])
        gate = ex / jnp.sum(ex, axis=0, keepdims=True)
        r0 = pl.multiple_of(h * PEER_TOPK, PEER_TOPK)
        e_scr[pl.ds(r0, PEER_TOPK), :] = expert
        g_scr[pl.ds(r0, PEER_TOPK), :] = gate
        return 0

    lax.fori_loop(0, n_heads, head_body, 0)
    for lb in range(n_lb):
        idx_ref[lb * LANES:(lb + 1) * LANES, :] = (
            e_scr[:, lb * LANES:(lb + 1) * LANES].T.astype(jnp.int32))
    gate_ref[...] = g_scr[...]


def _route(h2, w_pq, sub_keys2, n_heads, tm):
    m, d = h2.shape
    nq = w_pq.shape[1]
    n_keys = sub_keys2.shape[1]
    ne = n_heads * PEER_TOPK
    assert ne == LANES and tm % LANES == 0 and m % tm == 0
    cpos = jnp.asarray(np.broadcast_to(_cand_layout()[:, None], (N_CAND, tm)).copy())
    kern = functools.partial(_route_kernel, n_heads=n_heads, tm=tm)
    return pl.pallas_call(
        kern,
        grid=(m // tm,),
        in_specs=[
            pl.BlockSpec((tm, d), lambda i: (i, 0)),
            _resident((d, nq), lambda i: (0, 0)),
            _resident((2 * n_heads, n_keys, HEAD_DIM), lambda i: (0, 0, 0)),
            pl.BlockSpec((N_CAND, tm), lambda i: (0, 0)),
        ],
        out_specs=[
            pl.BlockSpec((tm, ne), lambda i: (i, 0)),
            pl.BlockSpec((ne, tm), lambda i: (0, i)),
        ],
        out_shape=[jax.ShapeDtypeStruct((m, ne), jnp.int32), jax.ShapeDtypeStruct((ne, m), f32)],
        scratch_shapes=[
            pltpu.VMEM((2 * n_heads, n_keys, tm), f32),
            pltpu.VMEM((ne, tm), f32),
            pltpu.VMEM((ne, tm), f32),
        ],
        compiler_params=_params(1, fuse_operand=1, n_operands=4),
        name="peer_route",
    )(h2, w_pq, sub_keys2, cpos)


SLAB = 16


def _pack_kernel(u_ref, v_ref, o_ref, scr):
    te = u_ref.shape[0]
    n_tiles = u_ref.shape[1] // LANES
    for half, src in enumerate((u_ref, v_ref)):
        for rg in range(te // 8):
            for j in range(n_tiles):
                scr[pl.ds(rg * 8 * n_tiles + j, 8, stride=n_tiles), :] = (
                    src[rg * 8:(rg + 1) * 8, j * LANES:(j + 1) * LANES])
        o_ref[:, half * n_tiles:(half + 1) * n_tiles, :] = (
            scr[...].reshape(te, n_tiles, LANES).astype(bf16))


def _pack_tables(u, v):
    n_exp, d = u.shape
    te = 256
    assert n_exp % te == 0 and d == SLAB * LANES
    return pl.pallas_call(
        _pack_kernel,
        grid=(n_exp // te,),
        in_specs=[pl.BlockSpec((te, d), lambda i: (i, 0)), pl.BlockSpec((te, d), lambda i: (i, 0))],
        out_specs=pl.BlockSpec((te, 2 * SLAB, LANES), lambda i: (i, 0, 0)),
        out_shape=jax.ShapeDtypeStruct((n_exp, 2 * SLAB, LANES), bf16),
        scratch_shapes=[pltpu.VMEM((te * SLAB, LANES), f32)],
        compiler_params=_params(1),
        name="pack_tables",
    )(u, v)


def _sublane_sums(r, sub):
    def merge(a, b, half):
        lo = (sub % (2 * half)) < half
        if 2 * half == 8:
            return jnp.where(lo, a, b) + pltpu.roll(jnp.where(lo, b, a), half, 0)
        x = jnp.where(lo, a, pltpu.roll(b, half, 0))
        y = jnp.where(lo, pltpu.roll(a, 8 - half, 0), b)
        return x + y
    a = [merge(r[j], r[j + 4], 4) for j in range(4)]
    b = [merge(a[j], a[j + 2], 2) for j in range(2)]
    return merge(b[0], b[1], 1)


def _expert_kernel(idx_hbm, uv_hbm, gate_ref, h2_ref, x1_ref, gt2_ref, o_ref, idx_smem, *rest, d, ne):
    rings, acts = rest[:RING], rest[RING:2 * RING]
    hs_scr, ys_scr, idx_sem, row_sem = rest[2 * RING:]
    step = pl.program_id(0)
    n_steps = pl.num_programs(0)
    n_tiles = d // LANES
    n_groups = ne // 8
    last_chunk = n_steps * (PEER_STEP // IDX_CHUNK) - 1
    lane = lax.broadcasted_iota(jnp.int32, (ne, PEER_STEP), 1)
    sub = lax.broadcasted_iota(jnp.int32, (8, LANES), 0)

    def idx_copy(chunk, buf):
        c = jnp.minimum(chunk, last_chunk)
        return pltpu.make_async_copy(idx_hbm.at[pl.ds(c * IDX_CHUNK, IDX_CHUNK)], idx_smem.at[buf],
                                     idx_sem.at[buf])

    def row_copy(t, e):
        buf, r = (t // IDX_CHUNK) % 2, t % IDX_CHUNK
        return pltpu.make_async_copy(uv_hbm.at[idx_smem[buf, r, e]], rings[t % RING].at[e],
                                     row_sem.at[t % RING])

    def wait_rows(slot):
        pltpu.make_async_copy(uv_hbm.at[pl.ds(0, ne)], rings[slot], row_sem.at[slot]).wait()

    @pl.when(step == 0)
    def _():
        idx_copy(0, 0).start()
        idx_copy(1, 1).start()
        idx_copy(0, 0).wait()
        idx_copy(1, 1).wait()

        def prime(e, _):
            for t in range(AHEAD):
                row_copy(t, e).start()
            return 0
        lax.fori_loop(0, ne, prime, 0)
        idx_copy(2, 0).start()

    hf = h2_ref[...].astype(f32)
    for rg in range(PEER_STEP // 8):
        for j in range(n_tiles):
            hs_scr[pl.ds(rg * 8 * n_tiles + j, 8, stride=n_tiles), :] = (
                hf[rg * 8:(rg + 1) * 8, j * LANES:(j + 1) * LANES])

    def ring_body(it, _):
        t0 = it * RING
        first_chunk = (step * PEER_STEP + t0) // IDX_CHUNK

        def start_copies(k, lo, hi):
            for e in range(lo, hi):
                row_copy(k + AHEAD, e).start(priority=e % 2)

        def open_token(k):
            if (k + AHEAD) % IDX_CHUNK == 0:
                q = (k + AHEAD) // IDX_CHUNK
                idx_copy(first_chunk + q, q % 2).wait()
                idx_copy(first_chunk + q + 1, 1 - q % 2).start()
            wait_rows(k)
            gcol = jnp.sum(jnp.where(lane == t0 + k, gate_ref[...], 0.0), axis=1, keepdims=True)
            hs = hs_scr[pl.ds(pl.multiple_of((t0 + k) * n_tiles, n_tiles), n_tiles), :]
            return gcol, hs

        def dot(k, hs, groups, n_copies):
            cols = []
            for g in groups:
                parts = []
                for j in range(8):
                    p = rings[k][g * 8 + j, 0:SLAB, :].astype(f32) * hs
                    parts.append(p[0:8, :] + p[8:16, :])
                cols.append(jnp.sum(_sublane_sums(parts, sub), axis=1, keepdims=True))
                start_copies(k, g * n_copies // n_groups, (g + 1) * n_copies // n_groups)
            return cols

        def activate(k, cols, gcol):
            s = jnp.concatenate(cols, axis=0)
            act = 0.5 * s * (1.0 + lax.erf(s * (2.0 ** -0.5))) * gcol
            acts[k][...] = jnp.broadcast_to(act, (ne, LANES))

        def axpy(k, stage, first_copy):
            accs = [jnp.zeros((SLAB, LANES), f32) for _ in range(4)]
            n_copies = ne - first_copy
            for g in range(n_groups):
                for e in range(g * 8, g * 8 + 8):
                    v = rings[k][e, SLAB:2 * SLAB, :].astype(f32)
                    accs[e % 4] = accs[e % 4] + acts[k][e:e + 1, :] * v
                start_copies(stage, first_copy + g * n_copies // n_groups,
                             first_copy + (g + 1) * n_copies // n_groups)
            ys_scr[pl.ds(pl.multiple_of((t0 + k) * n_tiles, n_tiles), n_tiles), :] = (
                (accs[0] + accs[1]) + (accs[2] + accs[3]))

        half = n_groups // 2
        gcol, hs = open_token(0)
        cols = dot(0, hs, range(n_groups), ne)
        for k in range(1, RING):
            gcol_k, hs = open_token(k)
            cols_k = dot(k, hs, range(half), DOT_COPIES)
            activate(k - 1, cols, gcol)
            cols_k += dot(k, hs, range(half, n_groups), DOT_COPIES)
            axpy(k - 1, k, DOT_COPIES)
            cols, gcol = cols_k, gcol_k
        activate(RING - 1, cols, gcol)
        axpy(RING - 1, RING - 1, ne)
        return 0

    lax.fori_loop(0, PEER_STEP // RING, ring_body, 0)

    for rg in range(PEER_STEP // 8):
        y = jnp.concatenate([ys_scr[pl.ds(rg * 8 * n_tiles + j, 8, stride=n_tiles), :]
                             for j in range(n_tiles)], axis=1)
        o_ref[rg * 8:(rg + 1) * 8, :] = x1_ref[rg * 8:(rg + 1) * 8, :] + gt2_ref[...] * y

    @pl.when(step == n_steps - 1)
    def _():
        for t in range(AHEAD):
            wait_rows(t % RING)
        last_fetch_stage = max(k for k in range(RING) if (k + AHEAD) % IDX_CHUNK == 0)
        idx_copy(last_chunk, 1 - ((last_fetch_stage + AHEAD) // IDX_CHUNK) % 2).wait()


def _experts(idx, gate_t, h2, x1, mod3, uv, seq):
    m, d = h2.shape
    ne = idx.shape[1]
    assert d == SLAB * LANES and uv.shape[1:] == (2 * SLAB, LANES) and ne % 8 == 0
    assert m % PEER_STEP == 0 and seq % PEER_STEP == 0
    assert RING % (2 * IDX_CHUNK) == 0 and PEER_STEP % RING == 0 and AHEAD <= 2 * IDX_CHUNK
    kern = functools.partial(_expert_kernel, d=d, ne=ne)
    return pl.pallas_call(
        kern,
        grid=(m // PEER_STEP,),
        in_specs=[
            pl.BlockSpec(memory_space=pl.ANY),
            pl.BlockSpec(memory_space=pl.ANY),
            pl.BlockSpec((ne, PEER_STEP), lambda i: (0, i)),
            pl.BlockSpec((PEER_STEP, d), lambda i: (i, 0)),
            pl.BlockSpec((PEER_STEP, d), lambda i: (i, 0)),
            pl.BlockSpec((None, 1, d), lambda i: ((i * PEER_STEP) // seq, 0, 5)),
        ],
        out_specs=pl.BlockSpec((PEER_STEP, d), lambda i: (i, 0)),
        out_shape=jax.ShapeDtypeStruct((m, d), f32),
        scratch_shapes=(
            [pltpu.SMEM((2, IDX_CHUNK, ne), jnp.int32)]
            + [pltpu.VMEM((ne, 2 * SLAB, LANES), bf16) for _ in range(RING)]
            + [pltpu.VMEM((ne, LANES), f32) for _ in range(RING)]
            + [pltpu.VMEM((PEER_STEP * SLAB, LANES), f32),
               pltpu.VMEM((PEER_STEP * SLAB, LANES), f32),
               pltpu.SemaphoreType.DMA((2,)),
               pltpu.SemaphoreType.DMA((RING,))]),
        compiler_params=_params(1),
        name="peer_experts",
    )(idx, uv, gate_t, h2, x1, mod3)


def kernel(x, c, ctx, c_ctx, w_ada, b_ada, g_norm1, w_in, q_norm, k_norm, rpb, w_conv,
           g_attn_out, g_conv_out, w_out, g_norm2, w_pq, sub_keys, u_experts, v_experts):
    batch, seq, d = x.shape
    ctx_len = ctx.shape[1]
    depth = w_ada.shape[0]
    n_heads = rpb.shape[1]
    d_attn = n_heads * HEAD_DIM
    d_conv = w_conv.shape[2]
    p_heads = sub_keys.shape[1]
    assert d % LANES == 0 and seq % (ATTN_Q_ROWS * GRID_W) == 0 and seq // GRID_W >= ATTN_K_ROWS
    assert w_in.shape[2] == 3 * d_attn + 3 * d_conv and g_conv_out.shape[2] == LANES
    assert sub_keys.shape[2] == 2 and sub_keys.shape[4] == HEAD_DIM
    tm = 256

    x2d = x.reshape(batch * seq, d)
    ctx2d = ctx.reshape(batch * ctx_len, d)
    mod_rows = -(-(batch + 1) // 8) * 8
    cvec = jnp.zeros((mod_rows, d), f32).at[:batch].set(c).at[batch].set(c_ctx)

    for l in range(depth):
        assert l == depth - 1
        mod = _ada(cvec, w_ada[l], b_ada[l][None, :])
        mod3 = mod.reshape(mod_rows, 1, mod.shape[1])

        w_in_b = w_in[l].astype(bf16)
        g1 = g_norm1[l][None, :]
        hg = jnp.concatenate([jnp.tile(q_norm[l], n_heads) * (HEAD_DIM ** -0.5),
                              jnp.tile(k_norm[l], n_heads)])[None, :]
        proj = _inproj(x2d, mod3, lambda i: (i * tm) // seq, g1, w_in_b, hg, 2 * n_heads, tm)
        hgc = jnp.tile(k_norm[l], n_heads)[None, :]
        kvc = _inproj(ctx2d, mod3, lambda i: batch, g1, w_in_b[:, d_attn:3 * d_attn], hgc,
                      n_heads, tm)

        bias = _attn_bias(rpb[l], seq // GRID_W)
        oattn = _attention(proj, kvc, bias, g_attn_out[l][:, None, :], batch, seq, ctx_len, n_heads)

        x1, h2 = _mix(oattn, proj, x2d, mod3, w_conv[l], g_conv_out[l].reshape(1, d_conv),
                      w_out[l].astype(bf16), g_norm2[l][None, :], seq, tm)

        sk2 = sub_keys[l].reshape(2 * p_heads, sub_keys.shape[3], HEAD_DIM).astype(bf16)
        idx, gate_t = _route(h2, w_pq[l].astype(bf16), sk2, p_heads, tm)

        x2d = _experts(idx, gate_t, h2, x1, mod3, _pack_tables(u_experts[l], v_experts[l]), seq)
    return x2d.reshape(batch, seq, d)
```
